```python
import math
import jax
import jax.numpy as jnp
from jax import lax
import numpy as np

D_MODEL = 2048
BATCH = 2
SEQ = 4096
DEPTH = 1
DEC_BATCH = 32
DEC_SEQ = 1
PAST_LEN = 8192
PAGE_SIZE = 128

SB_HEADS = 16
SB_HEAD_DIM = 128
SB_WIDTH = SB_HEADS * SB_HEAD_DIM
SB_QBLOCK = 128
SSM_EXPAND = 2
SSM_INNER = SSM_EXPAND * D_MODEL
SSM_HEAD_DIM = 64
SSM_HEADS = SSM_INNER // SSM_HEAD_DIM
SSM_GROUPS = 8
SSM_HPG = SSM_HEADS // SSM_GROUPS
SSM_STATE = 128
SSM_CONV = 4
SSM_CHUNK = 128
SSM_CONV_DIM = SSM_INNER + 2 * SSM_GROUPS * SSM_STATE
D_FF = 5632
FFN_CONV = 3
PLE_DIM = 256
EPS = 1e-6

OFF_Q = 0
OFF_K = OFF_Q + SB_WIDTH
OFF_V = OFF_K + SB_WIDTH
OFF_Z = OFF_V + SB_WIDTH
OFF_XBC = OFF_Z + SSM_INNER
OFF_DT = OFF_XBC + SSM_CONV_DIM
OFF_GA = OFF_DT + SSM_HEADS
OFF_GB = OFF_GA + D_MODEL
N_IN = OFF_GB + D_MODEL

kernel_name = 'stickbreak_ssd_hybrid_step'


def rms_norm(x, g):
    xf = x.astype(jnp.float32)
    y = xf * lax.rsqrt(jnp.mean(xf * xf, axis=-1, keepdims=True) + EPS)
    return (y * g.astype(jnp.float32)).astype(x.dtype)


def gated_group_rms_norm(y, z, g):
    u = y * jax.nn.silu(z.astype(jnp.float32))
    shp = u.shape
    ug = u.reshape(shp[:-1] + (SSM_GROUPS, SSM_INNER // SSM_GROUPS))
    ug = ug * lax.rsqrt(jnp.mean(ug * ug, axis=-1, keepdims=True) + EPS)
    return (ug.reshape(shp) * g.astype(jnp.float32)).astype(z.dtype)


def causal_dwconv(x, state, w, b):
    width = w.shape[0]
    s = x.shape[1]
    xc = jnp.concatenate([state.astype(x.dtype), x], axis=1)
    out = b.astype(x.dtype)
    for j in range(width):
        out = out + xc[:, j:j + s] * w[j]
    return out, xc[:, s:]


def stick_breaking(q, k, v, q_pos, k_pos, bias):
    b, sq, h, d = q.shape
    nblk = -(-sq // SB_QBLOCK)
    pad = nblk * SB_QBLOCK - sq
    qp = jnp.pad(q, ((0, 0), (0, pad), (0, 0), (0, 0)))
    posp = jnp.pad(q_pos, (0, pad), constant_values=-1)
    qb = qp.reshape(b, nblk, SB_QBLOCK, h, d).transpose(1, 0, 2, 3, 4)
    pb = posp.reshape(nblk, SB_QBLOCK)
    scale = 1.0 / math.sqrt(d)
    bias_f = bias.astype(jnp.float32)[None, :, None, None]

    def one_block(args):
        qi, pi = args
        z = jnp.einsum('bqhd,bkhd->bhqk', qi, k, preferred_element_type=jnp.float32) * scale + bias_f
        mask = k_pos[None, :] < pi[:, None]
        log_1mb = jnp.where(mask, jax.nn.log_sigmoid(-z), 0.0)
        tail = lax.cumsum(log_1mb, axis=3, reverse=True) - log_1mb
        w = jnp.where(mask, jnp.exp(jax.nn.log_sigmoid(z) + tail), 0.0)
        return jnp.einsum('bhqk,bkhd->bqhd', w.astype(v.dtype), v)

    out = lax.map(one_block, (qb, pb))
    return out.transpose(1, 0, 2, 3, 4).reshape(b, nblk * SB_QBLOCK, h, d)[:, :sq]


def ssd_chunked(xh, dt, a, bm, cm, h0):
    b, s = xh.shape[:2]
    L = min(SSM_CHUNK, s)
    nc = -(-s // L)
    pad = nc * L - s

    def padt(t):
        return jnp.pad(t, [(0, 0), (0, pad)] + [(0, 0)] * (t.ndim - 2))

    xs = padt(xh).reshape(b, nc, L, SSM_GROUPS, SSM_HPG, SSM_HEAD_DIM).transpose(1, 0, 2, 3, 4, 5)
    dts = padt(dt).reshape(b, nc, L, SSM_GROUPS, SSM_HPG).transpose(1, 0, 2, 3, 4)
    bs = padt(bm).reshape(b, nc, L, SSM_GROUPS, SSM_STATE).transpose(1, 0, 2, 3, 4)
    cs = padt(cm).reshape(b, nc, L, SSM_GROUPS, SSM_STATE).transpose(1, 0, 2, 3, 4)
    a_g = a.reshape(SSM_GROUPS, SSM_HPG)
    tri = jnp.tril(jnp.ones((L, L), dtype=bool))[None, :, :, None, None]

    def step(h, inp):
        xc, dtc, bc, cc = inp
        bc = bc.astype(jnp.float32)
        cc = cc.astype(jnp.float32)
        dA = jnp.cumsum(dtc * a_g, axis=1)
        diff = dA[:, :, None] - dA[:, None, :]
        decay = jnp.exp(jnp.where(tri, diff, -jnp.inf))
        cb = jnp.einsum('blgn,bsgn->blsg', cc, bc)
        xdt = xc.astype(jnp.float32) * dtc[..., None]
        y_intra = jnp.einsum('blsgr,bsgrp->blgrp', cb[..., None] * decay, xdt)
        y_inter = jnp.einsum('blgn,bgrpn->blgrp', cc, h) * jnp.exp(dA)[..., None]
        last = dA[:, -1]
        w_in = jnp.exp(last[:, None] - dA)
        h_new = h * jnp.exp(last)[..., None, None] + jnp.einsum('bsgn,bsgr,bsgrp->bgrpn', bc, w_in, xdt)
        return h_new, y_intra + y_inter

    h0g = h0.astype(jnp.float32).reshape(b, SSM_GROUPS, SSM_HPG, SSM_HEAD_DIM, SSM_STATE)
    hT, ys = lax.scan(step, h0g, (xs, dts, bs, cs))
    ys = ys.transpose(1, 0, 2, 3, 4, 5).reshape(b, nc * L, SSM_HEADS, SSM_HEAD_DIM)[:, :s]
    return ys, hT.reshape(b, SSM_HEADS, SSM_HEAD_DIM, SSM_STATE)


def layer(x, p_emb, k_past, v_past, ssm0, conv0, fconv0, w):
    b, s, _ = x.shape
    past = k_past.shape[1]
    h = rms_norm(x, w['g_mix'])
    proj = h @ w['w_in']
    q = rms_norm(proj[..., OFF_Q:OFF_K].reshape(b, s, SB_HEADS, SB_HEAD_DIM), w['g_q'])
    k = rms_norm(proj[..., OFF_K:OFF_V].reshape(b, s, SB_HEADS, SB_HEAD_DIM), w['g_k'])
    v = proj[..., OFF_V:OFF_Z].reshape(b, s, SB_HEADS, SB_HEAD_DIM)
    z = proj[..., OFF_Z:OFF_XBC]
    xbc = proj[..., OFF_XBC:OFF_DT]
    dt_raw = proj[..., OFF_DT:OFF_GA]
    gate_a = proj[..., OFF_GA:OFF_GB]
    gate_b = proj[..., OFF_GB:N_IN]

    k_all = jnp.concatenate([k_past.astype(k.dtype), k], axis=1)
    v_all = jnp.concatenate([v_past.astype(v.dtype), v], axis=1)
    q_pos = past + jnp.arange(s, dtype=jnp.int32)
    k_pos = jnp.arange(past + s, dtype=jnp.int32)
    o_a = stick_breaking(q, k_all, v_all, q_pos, k_pos, w['sb_bias']).reshape(b, s, SB_WIDTH)

    xbc_c, conv_new = causal_dwconv(xbc, conv0, w['conv_w'], w['conv_b'])
    xbc_c = jax.nn.silu(xbc_c)
    xh = xbc_c[..., :SSM_INNER].reshape(b, s, SSM_HEADS, SSM_HEAD_DIM)
    bm = xbc_c[..., SSM_INNER:SSM_INNER + SSM_GROUPS * SSM_STATE].reshape(b, s, SSM_GROUPS, SSM_STATE)
    cm = xbc_c[..., SSM_INNER + SSM_GROUPS * SSM_STATE:].reshape(b, s, SSM_GROUPS, SSM_STATE)
    dt = jax.nn.softplus(dt_raw.astype(jnp.float32) + w['dt_bias'].astype(jnp.float32))
    a = -jnp.exp(w['a_log'].astype(jnp.float32))
    y_ssm, ssm_new = ssd_chunked(xh, dt, a, bm, cm, ssm0)
    y_ssm = y_ssm + xh.astype(jnp.float32) * w['d_skip'].astype(jnp.float32)[:, None]
    o_b = gated_group_rms_norm(y_ssm.reshape(b, s, SSM_INNER), z, w['g_ssm'])

    merged = jax.nn.sigmoid(gate_a) * (o_a @ w['w_proj_a']) + jax.nn.sigmoid(gate_b) * (o_b @ w['w_proj_b'])
    x = x + merged @ w['w_out']

    up = rms_norm(x, w['g_ffn']) @ w['w_up']
    up_c, fconv_new = causal_dwconv(up, fconv0, w['ffn_conv_w'], w['ffn_conv_b'])
    x = x + (jax.nn.silu(up_c[..., :D_FF]) * up_c[..., D_FF:]) @ w['w_down']

    x = x + (p_emb @ w['w_ple']) * jax.nn.sigmoid(rms_norm(x, w['g_ple']) @ w['w_ple_gate'])
    return x, k, v, ssm_new, conv_new, fconv_new


def setup_inputs(seed: int = 0) -> dict:
    key = jax.random.key(seed)
    ks = iter(jax.random.split(key, 40))
    f32 = jnp.float32
    n_pages = PAST_LEN // PAGE_SIZE
    n_used = DEC_BATCH * n_pages
    n_pool = n_used + max(1, n_used // 4)

    def nrm(shape, scale=1.0):
        return jax.random.normal(next(ks), shape, f32) * scale

    def gain(shape):
        return 1.0 + nrm(shape, 0.1)

    x_prompt = nrm((BATCH, SEQ, D_MODEL))
    x_sample = nrm((DEC_BATCH, DEC_SEQ, D_MODEL))
    cache_k = nrm((DEPTH, n_pool, PAGE_SIZE, SB_HEADS, SB_HEAD_DIM))
    cache_v = nrm((DEPTH, n_pool, PAGE_SIZE, SB_HEADS, SB_HEAD_DIM))
    state_ssm = nrm((DEPTH, DEC_BATCH, SSM_HEADS, SSM_HEAD_DIM, SSM_STATE), 0.3)
    state_conv = nrm((DEPTH, DEC_BATCH, SSM_CONV - 1, SSM_CONV_DIM))
    state_ffn_conv = nrm((DEPTH, DEC_BATCH, FFN_CONV - 1, 2 * D_FF))
    page_table = jax.random.permutation(next(ks), n_pool)[:n_used].reshape(DEC_BATCH, n_pages).astype(jnp.int32)
    p_prompt = nrm((DEPTH, BATCH, SEQ, PLE_DIM))
    p_sample = nrm((DEPTH, DEC_BATCH, DEC_SEQ, PLE_DIM))

    dt0 = jnp.exp(jax.random.uniform(next(ks), (DEPTH, SSM_HEADS), f32, math.log(1e-3), math.log(1e-1)))
    dt_bias = dt0 + jnp.log(-jnp.expm1(-dt0))
    a_log = jnp.log(jax.random.uniform(next(ks), (DEPTH, SSM_HEADS), f32, 1.0, 16.0))
    sb_bias = jax.random.uniform(next(ks), (DEPTH, SB_HEADS), f32, -8.0, -5.0)

    return {
        'x_prompt': x_prompt, 'x_sample': x_sample,
        'cache_k': cache_k, 'cache_v': cache_v,
        'state_ssm': state_ssm, 'state_conv': state_conv, 'state_ffn_conv': state_ffn_conv,
        'page_table': page_table, 'p_prompt': p_prompt, 'p_sample': p_sample,
        'g_mix': gain((DEPTH, D_MODEL)),
        'w_in': nrm((DEPTH, D_MODEL, N_IN), D_MODEL ** -0.5),
        'g_q': gain((DEPTH, SB_HEAD_DIM)),
        'g_k': gain((DEPTH, SB_HEAD_DIM)),
        'sb_bias': sb_bias,
        'conv_w': nrm((DEPTH, SSM_CONV, SSM_CONV_DIM), SSM_CONV ** -0.5),
        'conv_b': nrm((DEPTH, SSM_CONV_DIM), 0.1),
        'dt_bias': dt_bias,
        'a_log': a_log,
        'd_skip': gain((DEPTH, SSM_HEADS)),
        'g_ssm': gain((DEPTH, SSM_INNER)),
        'w_proj_a': nrm((DEPTH, SB_WIDTH, D_MODEL), SB_WIDTH ** -0.5),
        'w_proj_b': nrm((DEPTH, SSM_INNER, D_MODEL), SSM_INNER ** -0.5),
        'w_out': nrm((DEPTH, D_MODEL, D_MODEL), D_MODEL ** -0.5),
        'g_ffn': gain((DEPTH, D_MODEL)),
        'w_up': nrm((DEPTH, D_MODEL, 2 * D_FF), D_MODEL ** -0.5),
        'ffn_conv_w': nrm((DEPTH, FFN_CONV, 2 * D_FF), FFN_CONV ** -0.5),
        'ffn_conv_b': nrm((DEPTH, 2 * D_FF), 0.1),
        'w_down': nrm((DEPTH, D_FF, D_MODEL), D_FF ** -0.5),
        'w_ple': nrm((DEPTH, PLE_DIM, D_MODEL), PLE_DIM ** -0.5),
        'g_ple': gain((DEPTH, D_MODEL)),
        'w_ple_gate': nrm((DEPTH, D_MODEL, D_MODEL), D_MODEL ** -0.5),
    }


def reference(x_prompt, x_sample, cache_k, cache_v, state_ssm, state_conv, state_ffn_conv, page_table,
              p_prompt, p_sample, g_mix, w_in, g_q, g_k, sb_bias, conv_w, conv_b, dt_bias, a_log, d_skip, g_ssm,
              w_proj_a, w_proj_b, w_out, g_ffn, w_up, ffn_conv_w, ffn_conv_b, w_down, w_ple, g_ple, w_ple_gate):
    bp = x_prompt.shape[0]
    db = x_sample.shape[0]
    xp = x_prompt
    xs = x_sample
    kp_l, vp_l, sp_l, cp_l, fp_l = [], [], [], [], []
    ks_l, vs_l, ss_l, cs_l, fs_l = [], [], [], [], []
    for i in range(DEPTH):
        w = dict(g_mix=g_mix[i], w_in=w_in[i], g_q=g_q[i], g_k=g_k[i], sb_bias=sb_bias[i],
                 conv_w=conv_w[i], conv_b=conv_b[i],
                 dt_bias=dt_bias[i], a_log=a_log[i], d_skip=d_skip[i], g_ssm=g_ssm[i],
                 w_proj_a=w_proj_a[i], w_proj_b=w_proj_b[i], w_out=w_out[i], g_ffn=g_ffn[i], w_up=w_up[i],
                 ffn_conv_w=ffn_conv_w[i], ffn_conv_b=ffn_conv_b[i], w_down=w_down[i], w_ple=w_ple[i],
                 g_ple=g_ple[i], w_ple_gate=w_ple_gate[i])
        empty = jnp.zeros((bp, 0, SB_HEADS, SB_HEAD_DIM), xp.dtype)
        xp, kp, vp, sp, cp, fp = layer(
            xp, p_prompt[i], empty, empty,
            jnp.zeros((bp, SSM_HEADS, SSM_HEAD_DIM, SSM_STATE), jnp.float32),
            jnp.zeros((bp, SSM_CONV - 1, SSM_CONV_DIM), xp.dtype),
            jnp.zeros((bp, FFN_CONV - 1, 2 * D_FF), xp.dtype), w)
        k_past = cache_k[i][page_table].reshape(db, -1, SB_HEADS, SB_HEAD_DIM)
        v_past = cache_v[i][page_table].reshape(db, -1, SB_HEADS, SB_HEAD_DIM)
        xs, kn, vn, sn, cn, fn = layer(xs, p_sample[i], k_past, v_past, state_ssm[i], state_conv[i],
                                       state_ffn_conv[i], w)
        kp_l.append(kp); vp_l.append(vp); sp_l.append(sp); cp_l.append(cp); fp_l.append(fp)
        ks_l.append(kn); vs_l.append(vn); ss_l.append(sn); cs_l.append(cn); fs_l.append(fn)
    return (xp, xs,
            jnp.stack(kp_l), jnp.stack(vp_l), jnp.stack(sp_l), jnp.stack(cp_l), jnp.stack(fp_l),
            jnp.stack(ks_l), jnp.stack(vs_l), jnp.stack(ss_l), jnp.stack(cs_l), jnp.stack(fs_l))
```

```python
import functools
import math

import jax
import jax.numpy as jnp
from jax import lax
from jax.experimental import pallas as pl
from jax.experimental.pallas import tpu as pltpu

F32 = jnp.float32
BF16 = jnp.bfloat16
EPS = 1e-6
LANES = 128
VMEM_LIMIT = 56 * 1024 * 1024
DT_PAD = 512


def _cparams(sem):
    return pltpu.CompilerParams(dimension_semantics=sem, vmem_limit_bytes=VMEM_LIMIT)


def _dot(a, b):
    return jnp.dot(a, b, preferred_element_type=F32)


def _dot_nt(a, b):
    return lax.dot_general(a, b, (((1,), (1,)), ((), ())), preferred_element_type=F32)


def _split3(x):
    hi = x.astype(BF16)
    r = x - hi.astype(F32)
    mid = r.astype(BF16)
    lo = (r - mid.astype(F32)).astype(BF16)
    return hi, mid, lo


def _dot3(x, m):
    hi, mid, lo = _split3(x)
    return _dot(hi, m) + _dot(mid, m) + _dot(lo, m)


def _dot3_r(m, x):
    hi, mid, lo = _split3(x)
    return _dot(m, hi) + _dot(m, mid) + _dot(m, lo)


def _softplus(x):
    return jnp.maximum(x, 0.0) + jnp.log1p(jnp.exp(-jnp.abs(x)))


def _silu(x):
    return x * jax.nn.sigmoid(x)


def _rms_rows(x, g):
    ms = jnp.mean(x * x, axis=-1, keepdims=True)
    return x * lax.rsqrt(ms + EPS) * g


def _rms_mm_kernel(x_ref, g_ref, w_ref, o_ref, xn_ref, *, row_chunk):
    @pl.when(pl.program_id(1) == 0)
    def _():
        tm = x_ref.shape[0]
        for c in range(tm // row_chunk):
            rows = pl.ds(c * row_chunk, row_chunk)
            xn_ref[rows, :] = _rms_rows(x_ref[rows, :], g_ref[...]).astype(BF16)

    o_ref[...] = _dot(xn_ref[...], w_ref[...]).astype(o_ref.dtype)


def _rms_matmul(x, g, w, *, tm, tn, out_dtype=F32):
    m, k = x.shape
    n = w.shape[1]
    tm = min(tm, m)
    row_chunk = min(256, tm)
    return pl.pallas_call(
        functools.partial(_rms_mm_kernel, row_chunk=row_chunk),
        grid=(m // tm, n // tn),
        in_specs=[
            pl.BlockSpec((tm, k), lambda i, j: (i, 0)),
            pl.BlockSpec((1, k), lambda i, j: (0, 0)),
            pl.BlockSpec((k, tn), lambda i, j: (0, j)),
        ],
        out_specs=pl.BlockSpec((tm, tn), lambda i, j: (i, j)),
        out_shape=jax.ShapeDtypeStruct((m, n), out_dtype),
        scratch_shapes=[pltpu.VMEM((tm, k), BF16)],
        compiler_params=_cparams(("parallel", "arbitrary")),
        name="rms_matmul",
    )(x, g.reshape(1, k), w)


def _qkv_prep_kernel(q_ref, k_ref, v_ref, gq_ref, gk_ref, qn_ref, kn_ref, knb_ref, vb_ref):
    qn_ref[...] = _rms_rows(q_ref[...], gq_ref[...]).astype(BF16)
    kn = _rms_rows(k_ref[...], gk_ref[...])
    kn_ref[...] = kn
    knb_ref[...] = kn.astype(BF16)
    vb_ref[...] = v_ref[...].astype(BF16)


def _qkv_prep(proj, g_q, g_k, *, heads, hd):
    m = proj.shape[0]
    tm = min(1024, m)
    width = heads * hd
    col = lambda off: pl.BlockSpec((tm, hd), lambda i, h: (i, off + h))
    vec = pl.BlockSpec((1, hd), lambda i, h: (0, 0))
    out = pl.BlockSpec((tm, hd), lambda i, h: (i, h))
    return pl.pallas_call(
        _qkv_prep_kernel,
        grid=(m // tm, heads),
        in_specs=[col(0), col(heads), col(2 * heads), vec, vec],
        out_specs=[out, out, out, out],
        out_shape=[
            jax.ShapeDtypeStruct((m, width), BF16),
            jax.ShapeDtypeStruct((m, width), F32),
            jax.ShapeDtypeStruct((m, width), BF16),
            jax.ShapeDtypeStruct((m, width), BF16),
        ],
        compiler_params=_cparams(("parallel", "parallel")),
        name="qkv_prep",
    )(proj, proj, proj, g_q.reshape(1, hd), g_k.reshape(1, hd))


def _sb_prompt_kernel(bias_ref, q_ref, k_ref, v_ref, tri_ref, o_ref, *, tq, tk, scale):
    h = pl.program_id(1)
    qi = pl.program_id(2)
    bias = bias_ref[h]
    q = q_ref[...]
    tri = tri_ref[...]

    def block(k0, carry, acc, masked):
        kb = k_ref[pl.ds(k0, tk), :]
        vb = v_ref[pl.ds(k0, tk), :]
        z = _dot_nt(q, kb) * scale + bias
        sp = _softplus(z)
        log_1mb = -sp
        if masked:
            row = lax.broadcasted_iota(jnp.int32, (tq, tk), 0)
            col = lax.broadcasted_iota(jnp.int32, (tq, tk), 1)
            keep = col < row
            log_1mb = jnp.where(keep, log_1mb, 0.0)
        hi = log_1mb.astype(BF16)
        lo = (log_1mb - hi.astype(F32)).astype(BF16)
        tail = _dot(hi, tri) + _dot(lo, tri)
        w = jnp.exp((z - sp) + tail + carry)
        if masked:
            w = jnp.where(keep, w, 0.0)
        acc = acc + _dot(w.astype(BF16), vb)
        carry = carry + jnp.sum(log_1mb, axis=1, keepdims=True)
        return carry, acc

    carry0 = jnp.zeros((tq, 1), F32)
    acc0 = jnp.zeros((tq, q.shape[1]), F32)
    carry, acc = block(pl.multiple_of(qi * tq, tq), carry0, acc0, True)

    def body(it, c):
        k0 = pl.multiple_of((qi - 1 - it) * tk, tk)
        return block(k0, c[0], c[1], False)

    carry, acc = lax.fori_loop(0, qi, body, (carry, acc))
    o_ref[...] = acc.astype(o_ref.dtype)


def _sb_prompt(qn, knb, vb, sb_bias, *, batch, seq, heads, hd):
    tq = tk = min(256, seq)
    assert tq == tk and seq % tq == 0
    width = heads * hd
    q3 = qn.reshape(batch, seq, width)
    k3 = knb.reshape(batch, seq, width)
    v3 = vb.reshape(batch, seq, width)
    tri = (jnp.arange(tk)[:, None] > jnp.arange(tk)[None, :]).astype(BF16)
    out = pl.pallas_call(
        functools.partial(_sb_prompt_kernel, tq=tq, tk=tk, scale=1.0 / math.sqrt(hd)),
        grid=(batch, heads, seq // tq),
        in_specs=[
            pl.BlockSpec(memory_space=pltpu.SMEM),
            pl.BlockSpec((None, tq, hd), lambda b, h, i: (b, i, h)),
            pl.BlockSpec((None, seq, hd), lambda b, h, i: (b, 0, h)),
            pl.BlockSpec((None, seq, hd), lambda b, h, i: (b, 0, h)),
            pl.BlockSpec((tk, tk), lambda b, h, i: (0, 0)),
        ],
        out_specs=pl.BlockSpec((None, tq, hd), lambda b, h, i: (b, i, h)),
        out_shape=jax.ShapeDtypeStruct((batch, seq, width), BF16),
        compiler_params=_cparams(("parallel", "parallel", "arbitrary")),
        name="sb_attn_prompt",
    )(sb_bias.astype(F32), q3, k3, v3, tri)
    return out.reshape(batch * seq, width)


def _sb_scores_kernel(pt_ref, q_ref, *rest, pages_per_step, heads, hd):
    k_refs = rest[:pages_per_step]
    z_ref = rest[pages_per_step]
    sq_ref = rest[pages_per_step + 1]

    @pl.when(pl.program_id(1) == 0)
    def _():
        width = heads * hd
        row = lax.broadcasted_iota(jnp.int32, (heads, width), 0)
        col = lax.broadcasted_iota(jnp.int32, (heads, width), 1)
        qb = jnp.broadcast_to(q_ref[...].astype(F32), (heads, width))
        sq_ref[...] = jnp.where(col // hd == row, qb, 0.0).astype(BF16)

    sq = sq_ref[...]
    for p in range(pages_per_step):
        z_ref[p] = _dot_nt(sq, k_refs[p][...].astype(BF16))


def _sb_scores(qn_s, cache_k2, page_table, *, heads, hd, pages_per_step):
    db, n_pages = page_table.shape
    page = cache_k2.shape[1]
    width = heads * hd
    steps = n_pages // pages_per_step

    def page_spec(p):
        return pl.BlockSpec((None, page, width),
                            lambda b, j, pt: (pt[b * n_pages + j * pages_per_step + p], 0, 0))

    return pl.pallas_call(
        functools.partial(_sb_scores_kernel, pages_per_step=pages_per_step, heads=heads, hd=hd),
        grid_spec=pltpu.PrefetchScalarGridSpec(
            num_scalar_prefetch=1,
            grid=(db, steps),
            in_specs=[pl.BlockSpec((None, 1, width), lambda b, j, pt: (b, 0, 0))]
            + [page_spec(p) for p in range(pages_per_step)],
            out_specs=pl.BlockSpec((None, pages_per_step, heads, page), lambda b, j, pt: (b, j, 0, 0)),
            scratch_shapes=[pltpu.VMEM((heads, width), BF16)],
        ),
        out_shape=jax.ShapeDtypeStruct((db, n_pages, heads, page), F32),
        compiler_params=_cparams(("parallel", "arbitrary")),
        name="sb_sample_scores",
    )(page_table.reshape(-1), qn_s.reshape(db, 1, width), *([cache_k2] * pages_per_step))


def _sb_weights_kernel(z_ref, bias_ref, tri_ref, ones_ref, mpage_ref, w_ref, *, scale):
    n_pages, heads, page = z_ref.shape
    z = z_ref[...] * scale + bias_ref[...]
    sp = _softplus(z)
    log_1mb = (-sp).reshape(n_pages * heads, page)
    log_b = (z - sp).reshape(n_pages * heads, page)
    within = _dot3(log_1mb, tri_ref[...])
    totals = _dot3(log_1mb, ones_ref[...])
    later = _dot3_r(mpage_ref[...], totals)
    w = jnp.exp(log_b + within + later)
    w_ref[...] = w.astype(BF16).reshape(n_pages, heads, page)


def _sb_weights(z, sb_bias, *, hd):
    db, n_pages, heads, page = z.shape
    rows = n_pages * heads
    tri = (jnp.arange(page)[:, None] > jnp.arange(page)[None, :]).astype(BF16)
    ones = jnp.ones((page, page), BF16)
    r = jnp.arange(rows)
    mpage = ((r[None, :] // heads > r[:, None] // heads) & (r[None, :] % heads == r[:, None] % heads)).astype(BF16)
    full = lambda shape: pl.BlockSpec(shape, lambda b: (0,) * len(shape))
    return pl.pallas_call(
        functools.partial(_sb_weights_kernel, scale=1.0 / math.sqrt(hd)),
        grid=(db,),
        in_specs=[
            pl.BlockSpec((None, n_pages, heads, page), lambda b: (b, 0, 0, 0)),
            full((1, heads, 1)),
            full((page, page)),
            full((page, page)),
            full((rows, rows)),
        ],
        out_specs=pl.BlockSpec((None, n_pages, heads, page), lambda b: (b, 0, 0, 0)),
        out_shape=jax.ShapeDtypeStruct((db, n_pages, heads, page), BF16),
        compiler_params=_cparams(("parallel",)),
        name="sb_sample_weights",
    )(z, sb_bias.astype(F32).reshape(1, heads, 1), tri, ones, mpage)


def _sb_values_kernel(pt_ref, w_ref, *rest, pages_per_step, heads, hd):
    v_refs = rest[:pages_per_step]
    o_ref = rest[pages_per_step]
    acc_ref = rest[pages_per_step + 1]
    j = pl.program_id(1)

    @pl.when(j == 0)
    def _():
        acc_ref[...] = jnp.zeros_like(acc_ref)

    acc = acc_ref[...]
    for p in range(pages_per_step):
        acc = acc + _dot(w_ref[p], v_refs[p][...].astype(BF16))
    acc_ref[...] = acc

    @pl.when(j == pl.num_programs(1) - 1)
    def _():
        width = heads * hd
        row = lax.broadcasted_iota(jnp.int32, (heads, width), 0)
        col = lax.broadcasted_iota(jnp.int32, (heads, width), 1)
        picked = jnp.where(col // hd == row, acc, 0.0)
        o_ref[...] = jnp.sum(picked, axis=0, keepdims=True).astype(o_ref.dtype)


def _sb_values(w, cache_v2, page_table, *, heads, hd, pages_per_step):
    db, n_pages = page_table.shape
    page = cache_v2.shape[1]
    width = heads * hd
    steps = n_pages // pages_per_step

    def page_spec(p):
        return pl.BlockSpec((None, page, width),
                            lambda b, j, pt: (pt[b * n_pages + j * pages_per_step + p], 0, 0))

    out = pl.pallas_call(
        functools.partial(_sb_values_kernel, pages_per_step=pages_per_step, heads=heads, hd=hd),
        grid_spec=pltpu.PrefetchScalarGridSpec(
            num_scalar_prefetch=1,
            grid=(db, steps),
            in_specs=[pl.BlockSpec((None, pages_per_step, heads, page), lambda b, j, pt: (b, j, 0, 0))]
            + [page_spec(p) for p in range(pages_per_step)],
            out_specs=pl.BlockSpec((None, 1, width), lambda b, j, pt: (b, 0, 0)),
            scratch_shapes=[pltpu.VMEM((heads, width), F32)],
        ),
        out_shape=jax.ShapeDtypeStruct((db, 1, width), BF16),
        compiler_params=_cparams(("parallel", "arbitrary")),
        name="sb_sample_values",
    )(page_table.reshape(-1), w, *([cache_v2] * pages_per_step))
    return out.reshape(db, width)


def _ssd_prompt_kernel(x_ref, bm_ref, cm_ref, z_ref, dtc_ref, dtr_ref,
                       wx_ref, wb_ref, wc_ref, bx_ref, bb_ref, bc_ref,
                       dtbc_ref, dtbr_ref, alc_ref, alr_ref, dskip_ref, gssm_ref,
                       tril_ref, triu_ref,
                       o_ref, st_ref,
                       ext_ref, ht_ref, *, chunk, hpg, pdim, nstate, conv_w):
    c = pl.program_id(2)
    gw = hpg * pdim
    halo = 8
    widths = (gw, nstate, nstate)
    offs = (0, gw, gw + nstate)

    @pl.when(c == 0)
    def _():
        ext_ref[0:halo, :] = jnp.zeros((halo, ext_ref.shape[1]), F32)
        ht_ref[...] = jnp.zeros_like(ht_ref)

    raws = (x_ref, bm_ref, cm_ref)
    wrefs = (wx_ref, wb_ref, wc_ref)
    brefs = (bx_ref, bb_ref, bc_ref)
    conv = []
    for raw, wref, bref, off, wd in zip(raws, wrefs, brefs, offs, widths):
        ext_ref[halo:halo + chunk, off:off + wd] = raw[...]
        acc = jnp.broadcast_to(bref[...], (chunk, wd))
        for j in range(conv_w):
            start = halo - (conv_w - 1) + j
            acc = acc + ext_ref[start:start + chunk, off:off + wd] * wref[j:j + 1, :]
        conv.append(_silu(acc))
    ext_ref[0:halo, :] = ext_ref[chunk:chunk + halo, :]
    xs, bmat, cmat = conv

    dt_c = _softplus(dtc_ref[...] + dtbc_ref[...])
    dt_r = _softplus(dtr_ref[...] + dtbr_ref[...])
    a_c = -jnp.exp(alc_ref[...])
    a_r = -jnp.exp(alr_ref[...])
    da_c = _dot3_r(tril_ref[...], dt_c * a_c)
    da_r = _dot3(dt_r * a_r, triu_ref[...])
    last_c = da_c[chunk - 1:chunk, :]
    exp_da_c = jnp.exp(da_c)
    w_in_c = jnp.exp(last_c - da_c)
    exp_last_c = jnp.exp(last_c)

    bmat_b = bmat.astype(BF16)
    cmat_b = cmat.astype(BF16)
    cb = _dot_nt(cmat_b, bmat_b)
    bmat_t = bmat.T.astype(BF16)

    row = lax.broadcasted_iota(jnp.int32, (chunk, chunk), 0)
    col = lax.broadcasted_iota(jnp.int32, (chunk, chunk), 1)
    causal = col <= row
    lane = lax.broadcasted_iota(jnp.int32, (chunk, LANES), 1)
    first = lane < pdim
    lane1 = lax.broadcasted_iota(jnp.int32, (1, LANES), 1)
    first1 = lane1 < pdim

    def pair_bcast(arr, r0):
        return jnp.where(first, arr[:, r0:r0 + 1], arr[:, r0 + 1:r0 + 2])

    heads_per_tile = LANES // pdim
    assert heads_per_tile == 2
    us = []
    ssq = jnp.zeros((chunk, 1), F32)
    for t in range(hpg // heads_per_tile):
        r0 = t * heads_per_tile
        lanes = slice(t * LANES, (t + 1) * LANES)
        decay = []
        for r in (r0, r0 + 1):
            diff = da_c[:, r:r + 1] - da_r[r:r + 1, :]
            decay.append((cb * jnp.where(causal, jnp.exp(jnp.where(causal, diff, 0.0)), 0.0)).astype(BF16))
        m_cat = jnp.concatenate(decay, axis=1)
        xp = xs[:, lanes]
        xdt = xp * pair_bcast(dt_c, r0)
        xbd = jnp.concatenate([jnp.where(first, xdt, 0.0), jnp.where(first, 0.0, xdt)], axis=0).astype(BF16)
        y = _dot(m_cat, xbd)
        ht_old = ht_ref[:, lanes]
        y = y + _dot(cmat_b, ht_old.astype(BF16)) * pair_bcast(exp_da_c, r0)
        xw = (xdt * pair_bcast(w_in_c, r0)).astype(BF16)
        scale_row = jnp.where(first1, exp_last_c[:, r0:r0 + 1], exp_last_c[:, r0 + 1:r0 + 2])
        ht_ref[:, lanes] = ht_old * scale_row + _dot(bmat_t, xw)
        y = y + xp * dskip_ref[:, lanes]
        u = y * _silu(z_ref[:, lanes])
        ssq = ssq + jnp.sum(u * u, axis=1, keepdims=True)
        us.append(u)
    inv = lax.rsqrt(ssq / gw + EPS)
    for t, u in enumerate(us):
        lanes = slice(t * LANES, (t + 1) * LANES)
        o_ref[:, lanes] = (u * inv * gssm_ref[:, lanes]).astype(o_ref.dtype)

    @pl.when(c == pl.num_programs(2) - 1)
    def _():
        st_ref[...] = ht_ref[...]


def _ssd_prompt(proj, dt_raw, conv_w, conv_b, dt_bias, a_log, d_skip, g_ssm, *,
                batch, seq, dims, offs):
    heads, pdim, nstate, groups = dims
    hpg = heads // groups
    gw = hpg * pdim
    inner = heads * pdim
    chunk = min(128, seq)
    n_chunks = seq // chunk
    off_z, off_xbc = offs
    cw = conv_w.shape[0]
    proj3 = proj.reshape(batch, seq, proj.shape[1])
    dt4 = dt_raw.reshape(batch, seq, groups, hpg)
    dtc = dt4.transpose(0, 2, 1, 3)
    dtr = dt4.transpose(0, 2, 3, 1)
    idx = jnp.arange(chunk)
    tril = (idx[:, None] >= idx[None, :]).astype(BF16)
    triu = (idx[:, None] <= idx[None, :]).astype(BF16)

    xcol0 = off_xbc // gw
    bcol0 = (off_xbc + inner) // nstate
    ccol0 = (off_xbc + inner + groups * nstate) // nstate
    zcol0 = off_z // gw
    wx0 = 0
    wb0 = inner // nstate
    wc0 = (inner + groups * nstate) // nstate

    def tok(width, col0):
        return pl.BlockSpec((None, chunk, width), lambda b, g, c: (b, c, col0 + g))

    def par(rows, width, col0):
        return pl.BlockSpec((rows, width), lambda b, g, c: (0, col0 + g))

    def grp(shape):
        return pl.BlockSpec((None,) + shape, lambda b, g, c: (g, 0, 0))

    full = pl.BlockSpec((chunk, chunk), lambda b, g, c: (0, 0))
    conv_b2 = conv_b.reshape(1, -1)
    o_b, st = pl.pallas_call(
        functools.partial(_ssd_prompt_kernel, chunk=chunk, hpg=hpg, pdim=pdim, nstate=nstate, conv_w=cw),
        grid=(batch, groups, n_chunks),
        in_specs=[
            tok(gw, xcol0), tok(nstate, bcol0), tok(nstate, ccol0), tok(gw, zcol0),
            pl.BlockSpec((None, None, chunk, hpg), lambda b, g, c: (b, g, c, 0)),
            pl.BlockSpec((None, None, hpg, chunk), lambda b, g, c: (b, g, 0, c)),
            par(cw, gw, wx0), par(cw, nstate, wb0), par(cw, nstate, wc0),
            par(1, gw, wx0), par(1, nstate, wb0), par(1, nstate, wc0),
            grp((1, hpg)), grp((hpg, 1)), grp((1, hpg)), grp((hpg, 1)),
            par(1, gw, 0), par(1, gw, 0),
            full, full,
        ],
        out_specs=[
            pl.BlockSpec((None, chunk, gw), lambda b, g, c: (b, c, g)),
            pl.BlockSpec((None, None, nstate, gw), lambda b, g, c: (b, g, 0, 0)),
        ],
        out_shape=[
            jax.ShapeDtypeStruct((batch, seq, inner), BF16),
            jax.ShapeDtypeStruct((batch, groups, nstate, gw), F32),
        ],
        scratch_shapes=[
            pltpu.VMEM((chunk + 8, gw + 2 * nstate), F32),
            pltpu.VMEM((nstate, gw), F32),
        ],
        compiler_params=_cparams(("parallel", "parallel", "arbitrary")),
        name="ssd_prompt",
    )(proj3, proj3, proj3, proj3, dtc, dtr,
      conv_w, conv_w, conv_w, conv_b2, conv_b2, conv_b2,
      dt_bias.reshape(groups, 1, hpg), dt_bias.reshape(groups, hpg, 1),
      a_log.reshape(groups, 1, hpg), a_log.reshape(groups, hpg, 1),
      jnp.repeat(d_skip, pdim).reshape(1, inner), g_ssm.reshape(1, inner),
      tril, triu)
    st = st.reshape(batch, groups, nstate, hpg, pdim).transpose(0, 1, 3, 4, 2)
    return o_b.reshape(batch * seq, inner), st.reshape(batch, heads, pdim, nstate)


def _ssd_sample_kernel(xbc_ref, z_ref, dt_ref, cst_ref, h0_ref,
                       cw_ref, cb_ref, dtb_ref, al_ref, dskip_ref, gssm_ref, expand_ref, eye_ref,
                       o_ref, cnew_ref, h_ref, *, inner, nstate, groups, conv_w):
    gw = inner // groups
    x = xbc_ref[...]
    acc = cb_ref[...] + x * cw_ref[conv_w - 1:conv_w, :]
    for j in range(conv_w - 1):
        acc = acc + cst_ref[j:j + 1, :] * cw_ref[j:j + 1, :]
        if j > 0:
            cnew_ref[j - 1:j, :] = cst_ref[j:j + 1, :]
    cnew_ref[conv_w - 2:conv_w - 1, :] = x
    xc = _silu(acc)
    xs = xc[:, :inner]
    bv = xc[:, inner:inner + groups * nstate]
    cv = xc[:, inner + groups * nstate:]

    dt8 = jnp.broadcast_to(dt_ref[...], (8, dt_ref.shape[1]))
    dt = _softplus(_dot3(dt8, expand_ref[...])[0:1, :] + dtb_ref[...])
    dec = jnp.exp(dt * (-jnp.exp(al_ref[...])))
    xdt = xs * dt
    eye = eye_ref[...]
    rows = eye.shape[0]
    ys = []
    for c in range(inner // rows):
        lanes = slice(c * rows, (c + 1) * rows)
        g = (c * rows) // gw
        glanes = slice(g * nstate, (g + 1) * nstate)
        hi, mid, lo = _split3(jnp.broadcast_to(xdt[:, lanes], (rows, rows)))
        xcol = _dot_nt(eye, hi) + _dot_nt(eye, mid) + _dot_nt(eye, lo)
        hi, mid, lo = _split3(jnp.broadcast_to(dec[:, lanes], (rows, rows)))
        dcol = _dot_nt(eye, hi) + _dot_nt(eye, mid) + _dot_nt(eye, lo)
        hn = h0_ref[lanes, :] * dcol + xcol * bv[:, glanes]
        h_ref[lanes, :] = hn
        cg = jnp.broadcast_to(cv[:, glanes], (8, nstate)).astype(BF16)
        ys.append(_dot_nt(cg, hn.astype(BF16))[0:1, :])
    y = jnp.concatenate(ys, axis=1) + xs * dskip_ref[...]
    u = y * _silu(z_ref[...])
    for g in range(groups):
        lanes = slice(g * gw, (g + 1) * gw)
        ug = u[:, lanes]
        inv = lax.rsqrt(jnp.mean(ug * ug, axis=1, keepdims=True) + EPS)
        o_ref[:, lanes] = (ug * inv * gssm_ref[:, lanes]).astype(o_ref.dtype)


def _ssd_sample(xbc, z, dt_raw, state_conv, state_ssm, conv_w, conv_b, dt_bias, a_log, d_skip, g_ssm, *, dims):
    heads, pdim, nstate, groups = dims
    inner = heads * pdim
    db, conv_dim = xbc.shape
    cw = conv_w.shape[0]
    rep = lambda v: jnp.repeat(v.astype(F32), pdim).reshape(1, inner)
    expand = (jnp.arange(heads)[:, None] == (jnp.arange(inner)[None, :] // pdim)).astype(BF16)
    eye = jnp.eye(LANES, dtype=BF16)
    row = lambda width: pl.BlockSpec((None, 1, width), lambda b: (b, 0, 0))
    full = lambda shape: pl.BlockSpec(shape, lambda b: (0,) * len(shape))
    o_b, conv_new, h_new = pl.pallas_call(
        functools.partial(_ssd_sample_kernel, inner=inner, nstate=nstate, groups=groups, conv_w=cw),
        grid=(db,),
        in_specs=[
            row(conv_dim), row(inner), row(heads),
            pl.BlockSpec((None, cw - 1, conv_dim), lambda b: (b, 0, 0)),
            pl.BlockSpec((None, inner, nstate), lambda b: (b, 0, 0)),
            full((cw, conv_dim)), full((1, conv_dim)),
            full((1, inner)), full((1, inner)), full((1, inner)), full((1, inner)),
            full((heads, inner)), full((LANES, LANES)),
        ],
        out_specs=[
            row(inner),
            pl.BlockSpec((None, cw - 1, conv_dim), lambda b: (b, 0, 0)),
            pl.BlockSpec((None, inner, nstate), lambda b: (b, 0, 0)),
        ],
        out_shape=[
            jax.ShapeDtypeStruct((db, 1, inner), BF16),
            jax.ShapeDtypeStruct((db, cw - 1, conv_dim), F32),
            jax.ShapeDtypeStruct((db, inner, nstate), F32),
        ],
        compiler_params=_cparams(("parallel",)),
        name="ssd_sample",
    )(xbc.reshape(db, 1, conv_dim), z.reshape(db, 1, inner), dt_raw.reshape(db, 1, heads),
      state_conv, state_ssm.reshape(db, inner, nstate),
      conv_w, conv_b.reshape(1, conv_dim), rep(dt_bias), rep(a_log), rep(d_skip), g_ssm.reshape(1, inner),
      expand, eye)
    return o_b.reshape(db, inner), conv_new, h_new.reshape(db, heads, pdim, nstate)


def _merge_kernel(oa_ref, ob_ref, wa_ref, wb_ref, ga_ref, gb_ref, o_ref):
    a = _dot(oa_ref[...], wa_ref[...])
    b = _dot(ob_ref[...], wb_ref[...])
    o_ref[...] = (jax.nn.sigmoid(ga_ref[...]) * a + jax.nn.sigmoid(gb_ref[...]) * b).astype(o_ref.dtype)


def _merge(o_a, o_b, wa, wb, proj, *, off_ga, off_gb, tm, tn):
    m, ka = o_a.shape
    kb = o_b.shape[1]
    n = wa.shape[1]
    tm = min(tm, m)
    tn = min(tn, n)
    ga0, gb0 = off_ga // tn, off_gb // tn
    return pl.pallas_call(
        _merge_kernel,
        grid=(m // tm, n // tn),
        in_specs=[
            pl.BlockSpec((tm, ka), lambda i, j: (i, 0)),
            pl.BlockSpec((tm, kb), lambda i, j: (i, 0)),
            pl.BlockSpec((ka, tn), lambda i, j: (0, j)),
            pl.BlockSpec((kb, tn), lambda i, j: (0, j)),
            pl.BlockSpec((tm, tn), lambda i, j: (i, ga0 + j)),
            pl.BlockSpec((tm, tn), lambda i, j: (i, gb0 + j)),
        ],
        out_specs=pl.BlockSpec((tm, tn), lambda i, j: (i, j)),
        out_shape=jax.ShapeDtypeStruct((m, n), BF16),
        compiler_params=_cparams(("parallel", "arbitrary")),
        name="merge",
    )(o_a, o_b, wa, wb, proj, proj)


def _resid_mm_kernel(x_ref, a_ref, w_ref, g_ref, o_ref, on_ref):
    y = x_ref[...] + _dot(a_ref[...], w_ref[...])
    o_ref[...] = y
    on_ref[...] = _rms_rows(y, g_ref[...]).astype(BF16)


def _resid_matmul_norm(x, a, w, g, *, tm):
    m, n = x.shape
    k = a.shape[1]
    tm = min(tm, m)
    return pl.pallas_call(
        _resid_mm_kernel,
        grid=(m // tm,),
        in_specs=[
            pl.BlockSpec((tm, n), lambda i: (i, 0)),
            pl.BlockSpec((tm, k), lambda i: (i, 0)),
            pl.BlockSpec((k, n), lambda i: (0, 0)),
            pl.BlockSpec((1, n), lambda i: (0, 0)),
        ],
        out_specs=[pl.BlockSpec((tm, n), lambda i: (i, 0)), pl.BlockSpec((tm, n), lambda i: (i, 0))],
        out_shape=[jax.ShapeDtypeStruct((m, n), F32), jax.ShapeDtypeStruct((m, n), BF16)],
        compiler_params=_cparams(("parallel",)),
        name="resid_matmul_norm",
    )(x, a, w, g.reshape(1, n))


def _resid_mm_tiled_kernel(x_ref, a_ref, w_ref, o_ref):
    o_ref[...] = x_ref[...] + _dot(a_ref[...], w_ref[...])


def _resid_matmul(x, a, w, *, tm, tn):
    m, n = x.shape
    k = a.shape[1]
    tm = min(tm, m)
    return pl.pallas_call(
        _resid_mm_tiled_kernel,
        grid=(m // tm, n // tn),
        in_specs=[
            pl.BlockSpec((tm, tn), lambda i, j: (i, j)),
            pl.BlockSpec((tm, k), lambda i, j: (i, 0)),
            pl.BlockSpec((k, tn), lambda i, j: (0, j)),
        ],
        out_specs=pl.BlockSpec((tm, tn), lambda i, j: (i, j)),
        out_shape=jax.ShapeDtypeStruct((m, n), F32),
        compiler_params=_cparams(("parallel", "arbitrary")),
        name="resid_matmul",
    )(x, a, w)


def _ffn_up_prompt_kernel(xn_ref, wa_ref, wb_ref, cwa_ref, cwb_ref, cba_ref, cbb_ref,
                          act_ref, ta_ref, tb_ref, exta_ref, extb_ref, *, seq, conv_w):
    i = pl.program_id(1)
    tm = xn_ref.shape[0]
    halo = 8
    xn = xn_ref[...]
    start_of_seq = (i * tm) % seq == 0
    outs = []
    for w_ref, cw_ref, cb_ref, ext_ref, tail_ref in ((wa_ref, cwa_ref, cba_ref, exta_ref, ta_ref),
                                                     (wb_ref, cwb_ref, cbb_ref, extb_ref, tb_ref)):
        up = _dot(xn, w_ref[...])

        @pl.when(start_of_seq)
        def _():
            ext_ref[0:halo, :] = jnp.zeros((halo, ext_ref.shape[1]), F32)

        ext_ref[halo:halo + tm, :] = up
        acc = jnp.broadcast_to(cb_ref[...], up.shape)
        for j in range(conv_w):
            start = halo - (conv_w - 1) + j
            acc = acc + ext_ref[start:start + tm, :] * cw_ref[j:j + 1, :]
        tail = ext_ref[tm:tm + halo, :]
        ext_ref[0:halo, :] = tail
        tail_ref[...] = tail
        outs.append(acc)
    act_ref[...] = (_silu(outs[0]) * outs[1]).astype(act_ref.dtype)


def _ffn_up_prompt(xn, w_up, cw, cb, *, seq, tm, tn):
    m, k = xn.shape
    n2 = w_up.shape[1]
    dff = n2 // 2
    tm = min(tm, m, seq)
    nj = dff // tn
    cwid = cw.shape[0]
    cb2 = cb.reshape(1, n2)
    act, ta, tb = pl.pallas_call(
        functools.partial(_ffn_up_prompt_kernel, seq=seq, conv_w=cwid),
        grid=(nj, m // tm),
        in_specs=[
            pl.BlockSpec((tm, k), lambda j, i: (i, 0)),
            pl.BlockSpec((k, tn), lambda j, i: (0, j)),
            pl.BlockSpec((k, tn), lambda j, i: (0, nj + j)),
            pl.BlockSpec((cwid, tn), lambda j, i: (0, j)),
            pl.BlockSpec((cwid, tn), lambda j, i: (0, nj + j)),
            pl.BlockSpec((1, tn), lambda j, i: (0, j)),
            pl.BlockSpec((1, tn), lambda j, i: (0, nj + j)),
        ],
        out_specs=[
            pl.BlockSpec((tm, tn), lambda j, i: (i, j)),
            pl.BlockSpec((None, 8, tn), lambda j, i: (i, 0, j)),
            pl.BlockSpec((None, 8, tn), lambda j, i: (i, 0, j)),
        ],
        out_shape=[
            jax.ShapeDtypeStruct((m, dff), BF16),
            jax.ShapeDtypeStruct((m // tm, 8, dff), F32),
            jax.ShapeDtypeStruct((m // tm, 8, dff), F32),
        ],
        scratch_shapes=[pltpu.VMEM((tm + 8, tn), F32), pltpu.VMEM((tm + 8, tn), F32)],
        compiler_params=_cparams(("parallel", "arbitrary")),
        name="ffn_up_prompt",
    )(xn, w_up, w_up, cw, cw, cb2, cb2)
    per_seq = (seq // tm)
    tails = jnp.concatenate([ta, tb], axis=-1)[per_seq - 1::per_seq, 8 - (cwid - 1):, :]
    return act, tails


def _ffn_up_sample_kernel(xn_ref, wa_ref, wb_ref, cwa_ref, cwb_ref, cba_ref, cbb_ref, sta_ref, stb_ref,
                          act_ref, ua_ref, ub_ref, *, conv_w):
    xn = xn_ref[...]
    outs = []
    for w_ref, cw_ref, cb_ref, st_ref, u_ref in ((wa_ref, cwa_ref, cba_ref, sta_ref, ua_ref),
                                                 (wb_ref, cwb_ref, cbb_ref, stb_ref, ub_ref)):
        up = _dot(xn, w_ref[...])
        u_ref[...] = up
        acc = cb_ref[...] + up * cw_ref[conv_w - 1:conv_w, :]
        for j in range(conv_w - 1):
            acc = acc + st_ref[j] * cw_ref[j:j + 1, :]
        outs.append(acc)
    act_ref[...] = (_silu(outs[0]) * outs[1]).astype(act_ref.dtype)


def _ffn_up_sample(xn, w_up, cw, cb, state, *, tn):
    m, k = xn.shape
    n2 = w_up.shape[1]
    dff = n2 // 2
    nj = dff // tn
    cwid = cw.shape[0]
    cb2 = cb.reshape(1, n2)
    st = state.transpose(1, 0, 2)
    act, ua, ub = pl.pallas_call(
        functools.partial(_ffn_up_sample_kernel, conv_w=cwid),
        grid=(nj,),
        in_specs=[
            pl.BlockSpec((m, k), lambda j: (0, 0)),
            pl.BlockSpec((k, tn), lambda j: (0, j)),
            pl.BlockSpec((k, tn), lambda j: (0, nj + j)),
            pl.BlockSpec((cwid, tn), lambda j: (0, j)),
            pl.BlockSpec((cwid, tn), lambda j: (0, nj + j)),
            pl.BlockSpec((1, tn), lambda j: (0, j)),
            pl.BlockSpec((1, tn), lambda j: (0, nj + j)),
            pl.BlockSpec((cwid - 1, m, tn), lambda j: (0, 0, j)),
            pl.BlockSpec((cwid - 1, m, tn), lambda j: (0, 0, nj + j)),
        ],
        out_specs=[pl.BlockSpec((m, tn), lambda j: (0, j))] * 3,
        out_shape=[
            jax.ShapeDtypeStruct((m, dff), BF16),
            jax.ShapeDtypeStruct((m, dff), F32),
            jax.ShapeDtypeStruct((m, dff), F32),
        ],
        compiler_params=_cparams(("parallel",)),
        name="ffn_up_sample",
    )(xn, w_up, w_up, cw, cw, cb2, cb2, st, st)
    up = jnp.concatenate([ua, ub], axis=-1)
    new_state = jnp.concatenate([state[:, 1:, :], up[:, None, :]], axis=1)
    return act, new_state


def _ple_kernel(x_ref, xt_ref, p_ref, g_ref, wp_ref, wg_ref, o_ref, xn_ref, *, row_chunk):
    @pl.when(pl.program_id(1) == 0)
    def _():
        tm = x_ref.shape[0]
        for c in range(tm // row_chunk):
            rows = pl.ds(c * row_chunk, row_chunk)
            xn_ref[rows, :] = _rms_rows(x_ref[rows, :], g_ref[...]).astype(BF16)

    emb = _dot(p_ref[...].astype(BF16), wp_ref[...])
    gate = _dot(xn_ref[...], wg_ref[...])
    o_ref[...] = xt_ref[...] + emb * jax.nn.sigmoid(gate)


def _ple(x, p, g, w_ple, w_gate, *, tm, tn):
    m, d = x.shape
    kp = p.shape[1]
    tm = min(tm, m)
    row_chunk = min(256, tm)
    return pl.pallas_call(
        functools.partial(_ple_kernel, row_chunk=row_chunk),
        grid=(m // tm, d // tn),
        in_specs=[
            pl.BlockSpec((tm, d), lambda i, j: (i, 0)),
            pl.BlockSpec((tm, tn), lambda i, j: (i, j)),
            pl.BlockSpec((tm, kp), lambda i, j: (i, 0)),
            pl.BlockSpec((1, d), lambda i, j: (0, 0)),
            pl.BlockSpec((kp, tn), lambda i, j: (0, j)),
            pl.BlockSpec((d, tn), lambda i, j: (0, j)),
        ],
        out_specs=pl.BlockSpec((tm, tn), lambda i, j: (i, j)),
        out_shape=jax.ShapeDtypeStruct((m, d), F32),
        scratch_shapes=[pltpu.VMEM((tm, d), BF16)],
        compiler_params=_cparams(("parallel", "arbitrary")),
        name="ple",
    )(x, x, p, g.reshape(1, d), w_ple, w_gate)


def kernel(x_prompt, x_sample, cache_k, cache_v, state_ssm, state_conv, state_ffn_conv, page_table,
           p_prompt, p_sample, g_mix, w_in, g_q, g_k, sb_bias, conv_w, conv_b, dt_bias, a_log, d_skip, g_ssm,
           w_proj_a, w_proj_b, w_out, g_ffn, w_up, ffn_conv_w, ffn_conv_b, w_down, w_ple, g_ple, w_ple_gate):
    depth = w_in.shape[0]
    assert depth == 1
    batch, seq, d_model = x_prompt.shape
    db, dseq, _ = x_sample.shape
    assert dseq == 1
    _, n_pool, page, sb_heads, hd = cache_k.shape
    sb_width = sb_heads * hd
    _, _, heads, pdim, nstate = state_ssm.shape
    inner = heads * pdim
    conv_dim = state_conv.shape[-1]
    groups = (conv_dim - inner) // (2 * nstate)
    dims = (heads, pdim, nstate, groups)
    d_ff = w_down.shape[1]

    off_z = 3 * sb_width
    off_xbc = off_z + inner
    off_dt = off_xbc + conv_dim
    off_ga_src = off_dt + heads
    off_ga = off_dt + DT_PAD
    off_gb = off_ga + d_model
    w_in0 = w_in[0]
    w_in_b = jnp.concatenate(
        [w_in0[:, :off_ga_src].astype(BF16),
         jnp.zeros((d_model, DT_PAD - heads), BF16),
         w_in0[:, off_ga_src:].astype(BF16)], axis=1)
    wa_b = w_proj_a[0].astype(BF16)
    wb_b = w_proj_b[0].astype(BF16)
    wout_b = w_out[0].astype(BF16)
    wup_b = w_up[0].astype(BF16)
    wdown_b = w_down[0].astype(BF16)
    wple_b = w_ple[0].astype(BF16)
    wgate_b = w_ple_gate[0].astype(BF16)

    def mixer_inputs(x2):
        proj = _rms_matmul(x2, g_mix[0], w_in_b, tm=1024, tn=512)
        qn, kn, knb, vb = _qkv_prep(proj, g_q[0], g_k[0], heads=sb_heads, hd=hd)
        v = proj[:, 2 * sb_width:3 * sb_width]
        dt_raw = proj[:, off_dt:off_dt + heads]
        return proj, qn, kn, knb, vb, v, dt_raw

    def channel_mix(x2, o_a, o_b, proj, p2, ffn_up):
        merged = _merge(o_a, o_b, wa_b, wb_b, proj, off_ga=off_ga, off_gb=off_gb, tm=1024, tn=512)
        x1, xn1 = _resid_matmul_norm(x2, merged, wout_b, g_ffn[0], tm=256)
        act, ffn_state = ffn_up(xn1)
        x2b = _resid_matmul(x1, act, wdown_b, tm=1024, tn=512)
        x3 = _ple(x2b, p2, g_ple[0], wple_b, wgate_b, tm=1024, tn=512)
        return x3, ffn_state

    xp2 = x_prompt.reshape(batch * seq, d_model)
    proj, qn, kn, knb, vb, v, dt_raw = mixer_inputs(xp2)
    o_a = _sb_prompt(qn, knb, vb, sb_bias[0], batch=batch, seq=seq, heads=sb_heads, hd=hd)
    o_b, ssm_p = _ssd_prompt(proj, dt_raw, conv_w[0], conv_b[0], dt_bias[0], a_log[0], d_skip[0], g_ssm[0],
                             batch=batch, seq=seq, dims=dims, offs=(off_z, off_xbc))
    cw = conv_w.shape[1]
    conv_p = proj.reshape(batch, seq, -1)[:, seq - (cw - 1):, off_xbc:off_dt]
    yp, ffn_p = channel_mix(
        xp2, o_a, o_b, proj, p_prompt[0].reshape(batch * seq, -1),
        lambda xn1: _ffn_up_prompt(xn1, wup_b, ffn_conv_w[0], ffn_conv_b[0], seq=seq, tm=1024, tn=512))
    k_p = kn.reshape(batch, seq, sb_heads, hd)
    v_p = v.reshape(batch, seq, sb_heads, hd)

    xs2 = x_sample.reshape(db, d_model)
    proj_s, qn_s, kn_s, _, _, v_s, dt_raw_s = mixer_inputs(xs2)
    ck2 = cache_k[0].reshape(n_pool, page, sb_width)
    cv2 = cache_v[0].reshape(n_pool, page, sb_width)
    n_pages = page_table.shape[1]
    pps = 8 if n_pages % 8 == 0 else 1
    z_s = _sb_scores(qn_s, ck2, page_table, heads=sb_heads, hd=hd, pages_per_step=pps)
    w_s = _sb_weights(z_s, sb_bias[0], hd=hd)
    o_a_s = _sb_values(w_s, cv2, page_table, heads=sb_heads, hd=hd, pages_per_step=pps)
    o_b_s, conv_s, ssm_s = _ssd_sample(
        proj_s[:, off_xbc:off_dt], proj_s[:, off_z:off_xbc], dt_raw_s, state_conv[0], state_ssm[0],
        conv_w[0], conv_b[0], dt_bias[0], a_log[0], d_skip[0], g_ssm[0], dims=dims)
    ys, ffn_s = channel_mix(
        xs2, o_a_s, o_b_s, proj_s, p_sample[0].reshape(db, -1),
        lambda xn1: _ffn_up_sample(xn1, wup_b, ffn_conv_w[0], ffn_conv_b[0], state_ffn_conv[0], tn=512))

    return (yp.reshape(batch, seq, d_model), ys.reshape(db, 1, d_model),
            k_p[None], v_p[None], ssm_p[None], conv_p[None], ffn_p[None],
            kn_s.reshape(db, 1, sb_heads, hd)[None], v_s.reshape(db, 1, sb_heads, hd)[None],
            ssm_s[None], conv_s[None], ffn_s[None])
```

```python
import functools
import math

import jax
import jax.numpy as jnp
from jax import lax
from jax.experimental import pallas as pl
from jax.experimental.pallas import tpu as pltpu

F32 = jnp.float32
BF16 = jnp.bfloat16
EPS = 1e-6
LANES = 128
SUBLANES = 8
VMEM_LIMIT = 56 * 1024 * 1024
COL_TILE = 512
ROW_TILE = 1024
PROJ_COL_TILE = 1024
ATTN_BLOCK = 256
ATTN_HEADS_PER_STEP = 4
PAGES_PER_STEP = 8


def _cparams(sem):
    return pltpu.CompilerParams(dimension_semantics=sem, vmem_limit_bytes=VMEM_LIMIT)


def _dot(a, b):
    return jnp.dot(a, b, preferred_element_type=F32)


def _dot_nt(a, b):
    return lax.dot_general(a, b, (((1,), (1,)), ((), ())), preferred_element_type=F32)


def _split3(x):
    hi = x.astype(BF16)
    r = x - hi.astype(F32)
    mid = r.astype(BF16)
    lo = (r - mid.astype(F32)).astype(BF16)
    return hi, mid, lo


def _dot3(x, m):
    hi, mid, lo = _split3(x)
    return _dot(hi, m) + _dot(mid, m) + _dot(lo, m)


def _dot3_r(m, x):
    hi, mid, lo = _split3(x)
    return _dot(m, hi) + _dot(m, mid) + _dot(m, lo)


def _softplus(x):
    return jnp.maximum(x, 0.0) + jnp.log(1.0 + jnp.exp(-jnp.abs(x)))


LOG2E = math.log2(math.e)


def _softplus2(x):
    neg_abs = lax.bitcast_convert_type(lax.bitcast_convert_type(x, jnp.uint32) | jnp.uint32(0x80000000), F32)
    return jnp.maximum(x, 0.0) + jnp.log2(1.0 + jnp.exp2(neg_abs))


def _silu(x):
    return x * jax.nn.sigmoid(x)


def _rms_rows(x, g):
    ms = jnp.mean(x * x, axis=-1, keepdims=True)
    return x * lax.rsqrt(ms + EPS) * g


def _rms_norm_kernel(x_ref, g_ref, o_ref):
    o_ref[...] = _rms_rows(x_ref[...], g_ref[...]).astype(BF16)


def _rms_norm_bf16(x, g):
    m, k = x.shape
    tm = min(256, m)
    return pl.pallas_call(
        _rms_norm_kernel,
        grid=(m // tm,),
        in_specs=[pl.BlockSpec((tm, k), lambda i: (i, 0)), pl.BlockSpec((1, k), lambda i: (0, 0))],
        out_specs=pl.BlockSpec((tm, k), lambda i: (i, 0)),
        out_shape=jax.ShapeDtypeStruct((m, k), BF16),
        compiler_params=_cparams(("parallel",)),
        name="rms_norm",
    )(x, g.reshape(1, k))


def _proj_kernel(x_ref, w_ref, *rest, n_extra, n_out, epilogue):
    extras = rest[:n_extra]
    outs = rest[n_extra:n_extra + n_out]
    wb_ref = rest[n_extra + n_out]

    @pl.when(pl.program_id(1) == 0)
    def _():
        wb_ref[...] = w_ref[...].astype(BF16)

    epilogue(_dot(x_ref[...], wb_ref[...]), *extras, *outs)


def _epi_plain(acc, o_ref):
    o_ref[...] = acc


def _epi_q(acc, g_ref, qn_ref, *, hd):
    scale = LOG2E / math.sqrt(hd)
    for h in range(acc.shape[1] // hd):
        lanes = slice(h * hd, (h + 1) * hd)
        qn_ref[:, lanes] = (_rms_rows(acc[:, lanes], g_ref[...]) * scale).astype(BF16)


def _epi_k(acc, g_ref, kn_ref, knb_ref, *, hd):
    for h in range(acc.shape[1] // hd):
        lanes = slice(h * hd, (h + 1) * hd)
        kn = _rms_rows(acc[:, lanes], g_ref[...])
        kn_ref[:, h, :] = kn
        knb_ref[:, lanes] = kn.astype(BF16)


def _epi_v(acc, v_ref, vbt_ref, *, hd):
    for h in range(acc.shape[1] // hd):
        lanes = slice(h * hd, (h + 1) * hd)
        v_ref[:, h, :] = acc[:, lanes]
        vbt_ref[lanes, :] = acc[:, lanes].T.astype(BF16)


def _proj(xn, w, *, col0, ncols, tn, epilogue, extras, out_kinds, hd, name):
    m, k = xn.shape
    tm = min(ROW_TILE, m)
    tn = min(tn, ncols)
    assert col0 % tn == 0 and ncols % tn == 0
    c0 = col0 // tn
    out_specs, out_shapes = [], []
    for kind in out_kinds:
        if kind == "heads":
            out_specs.append(pl.BlockSpec((tm, tn // hd, hd), lambda j, i: (i, j, 0)))
            out_shapes.append(jax.ShapeDtypeStruct((m, ncols // hd, hd), F32))
        elif kind == "bf16_t":
            out_specs.append(pl.BlockSpec((tn, tm), lambda j, i: (j, i)))
            out_shapes.append(jax.ShapeDtypeStruct((ncols, m), BF16))
        else:
            out_specs.append(pl.BlockSpec((tm, tn), lambda j, i: (i, j)))
            out_shapes.append(jax.ShapeDtypeStruct((m, ncols), F32 if kind == "f32" else BF16))
    extra_specs = [pl.BlockSpec(e.shape, lambda j, i: (0, 0)) for e in extras]
    return pl.pallas_call(
        functools.partial(_proj_kernel, n_extra=len(extras), n_out=len(out_kinds), epilogue=epilogue),
        grid=(ncols // tn, m // tm),
        in_specs=[pl.BlockSpec((tm, k), lambda j, i: (i, 0)),
                  pl.BlockSpec((k, tn), lambda j, i: (0, c0 + j))] + extra_specs,
        out_specs=out_specs,
        out_shape=out_shapes,
        scratch_shapes=[pltpu.VMEM((k, tn), BF16)],
        compiler_params=_cparams(("parallel", "arbitrary")),
        name=name,
    )(xn, w, *extras)


def _sb_prompt_kernel(bias_ref, q_ref, k_ref, vt_ref, tri_ref, o_ref, *, tq, tk, hd, hps):
    hb = pl.program_id(1)
    qi = pl.program_id(2)
    tri = tri_ref[...]

    def block(k0, state, masked):
        if masked:
            row = lax.broadcasted_iota(jnp.int32, (tq, tk), 0)
            col = lax.broadcasted_iota(jnp.int32, (tq, tk), 1)
            keep = col < row
        heads = range(hps)
        lanes = [slice(g * hd, (g + 1) * hd) for g in heads]
        zs = [_dot_nt(q_ref[:, lanes[g]], k_ref[pl.ds(k0, tk), lanes[g]]) + bias_ref[hb * hps + g] for g in heads]
        sps = [_softplus2(z) for z in zs]
        if masked:
            sps = [jnp.where(keep, sp, 0.0) for sp in sps]
        spb = [sp.astype(BF16) for sp in sps]
        tails = [_dot(sb, tri) for sb in spb]
        ws = [jnp.exp2(zs[g] - sps[g] - tails[g] - state[g][0]) for g in heads]
        if masked:
            ws = [jnp.where(keep, w, 0.0) for w in ws]
        pvs = [_dot_nt(vt_ref[lanes[g], pl.ds(k0, tk)], ws[g].astype(BF16)) for g in heads]
        sums = [tails[g][:, 0:1] + spb[g][:, 0:1].astype(F32) for g in heads]
        return tuple((state[g][0] + sums[g], state[g][1] + pvs[g]) for g in heads)

    state = tuple((jnp.zeros((tq, 1), F32), jnp.zeros((hd, tq), F32)) for _ in range(hps))
    state = block(pl.multiple_of(qi * tq, tq), state, True)

    def body(it, st):
        return block(pl.multiple_of((qi - 1 - it) * tk, tk), st, False)

    state = lax.fori_loop(0, qi, body, state)
    for g in range(hps):
        o_ref[:, g * hd:(g + 1) * hd] = state[g][1].T.astype(o_ref.dtype)


def _sb_prompt(qn, knb, vbt, sb_bias, *, batch, seq, heads, hd):
    tq = tk = min(ATTN_BLOCK, seq)
    hps = ATTN_HEADS_PER_STEP
    assert seq % tq == 0 and heads % hps == 0
    width = heads * hd
    q3 = qn.reshape(batch, seq, width)
    k3 = knb.reshape(batch, seq, width)
    tri = (jnp.arange(tk)[:, None] > jnp.arange(tk)[None, :]).astype(BF16)
    out = pl.pallas_call(
        functools.partial(_sb_prompt_kernel, tq=tq, tk=tk, hd=hd, hps=hps),
        grid=(batch, heads // hps, seq // tq),
        in_specs=[
            pl.BlockSpec(memory_space=pltpu.SMEM),
            pl.BlockSpec((None, tq, hps * hd), lambda b, h, i: (b, i, h)),
            pl.BlockSpec((None, seq, hps * hd), lambda b, h, i: (b, 0, h)),
            pl.BlockSpec((hps * hd, seq), lambda b, h, i: (h, b)),
            pl.BlockSpec((tk, tk), lambda b, h, i: (0, 0)),
        ],
        out_specs=pl.BlockSpec((None, tq, hps * hd), lambda b, h, i: (b, i, h)),
        out_shape=jax.ShapeDtypeStruct((batch, seq, width), BF16),
        compiler_params=_cparams(("parallel", "parallel", "arbitrary")),
        name="sb_attn_prompt",
    )(sb_bias.astype(F32) * LOG2E, q3, k3, vbt, tri)
    return out.reshape(batch * seq, width)


def _page_spec(p, n_pages, pages_per_step, page, heads, hd):
    return pl.BlockSpec((None, None, page, heads, hd),
                        lambda b, j, pt: (0, pt[b * n_pages + j * pages_per_step + p], 0, 0, 0))


def _own_head(heads, width):
    row = lax.broadcasted_iota(jnp.int32, (heads, width), 0)
    col = lax.broadcasted_iota(jnp.int32, (heads, width), 1)
    return col % heads == row


def _sb_scores_kernel(pt_ref, q_ref, *rest, pages_per_step, heads):
    k_refs = rest[:pages_per_step]
    z_ref = rest[pages_per_step]
    q = q_ref[...]
    own = _own_head(heads, z_ref.shape[1])
    rows = []
    for p in range(pages_per_step):
        kp = k_refs[p][...]
        kp = kp.reshape(kp.shape[0] * heads, kp.shape[2]).astype(BF16)
        full = _dot_nt(q, kp)
        rows.append(jnp.sum(jnp.where(own, full, 0.0), axis=0, keepdims=True))
    z_ref[...] = jnp.concatenate(rows, axis=0)


def _sb_scores(qn_s, cache_k, page_table, *, pages_per_step):
    db, n_pages = page_table.shape
    _, _, page, heads, hd = cache_k.shape
    steps = n_pages // pages_per_step
    return pl.pallas_call(
        functools.partial(_sb_scores_kernel, pages_per_step=pages_per_step, heads=heads),
        grid_spec=pltpu.PrefetchScalarGridSpec(
            num_scalar_prefetch=1,
            grid=(db, steps),
            in_specs=[pl.BlockSpec((None, heads, hd), lambda b, j, pt: (b, 0, 0))]
            + [_page_spec(p, n_pages, pages_per_step, page, heads, hd) for p in range(pages_per_step)],
            out_specs=pl.BlockSpec((None, pages_per_step, page * heads), lambda b, j, pt: (b, j, 0)),
        ),
        out_shape=jax.ShapeDtypeStruct((db, n_pages, page * heads), F32),
        compiler_params=_cparams(("parallel", "arbitrary")),
        name="sb_sample_scores",
    )(page_table.reshape(-1), qn_s.reshape(db, heads, hd), *([cache_k] * pages_per_step))


def _sb_weights_kernel(z_ref, bias_ref, tri_ref, ones_ref, mpage_ref, w_ref):
    n_pages, heads, page = z_ref.shape
    z = z_ref[...] + bias_ref[...]
    sp = _softplus2(z)
    log_1mb = (-sp).reshape(n_pages * heads, page)
    log_b = (z - sp).reshape(n_pages * heads, page)
    within = _dot3(log_1mb, tri_ref[...])
    totals = _dot3(log_1mb, ones_ref[...])
    later = _dot3_r(mpage_ref[...], totals)
    w = jnp.exp2(log_b + within + later)
    w_ref[...] = w.reshape(n_pages, heads, page)


def _sb_weights(z, sb_bias):
    db, n_pages, heads, page = z.shape
    rows = n_pages * heads
    tri = (jnp.arange(page)[:, None] > jnp.arange(page)[None, :]).astype(BF16)
    ones = jnp.ones((page, page), BF16)
    r = jnp.arange(rows)
    mpage = ((r[None, :] // heads > r[:, None] // heads) & (r[None, :] % heads == r[:, None] % heads)).astype(BF16)
    full = lambda shape: pl.BlockSpec(shape, lambda b: (0,) * len(shape))
    return pl.pallas_call(
        _sb_weights_kernel,
        grid=(db,),
        in_specs=[
            pl.BlockSpec((None, n_pages, heads, page), lambda b: (b, 0, 0, 0)),
            full((1, heads, 1)),
            full((page, page)),
            full((page, page)),
            full((rows, rows)),
        ],
        out_specs=pl.BlockSpec((None, n_pages, heads, page), lambda b: (b, 0, 0, 0)),
        out_shape=jax.ShapeDtypeStruct((db, n_pages, heads, page), F32),
        compiler_params=_cparams(("parallel",)),
        name="sb_sample_weights",
    )(z, (sb_bias.astype(F32) * LOG2E).reshape(1, heads, 1), tri, ones, mpage)


def _sb_values_kernel(pt_ref, w_ref, *rest, pages_per_step, heads):
    v_refs = rest[:pages_per_step]
    o_ref = rest[pages_per_step]
    acc_ref = rest[pages_per_step + 1]
    j = pl.program_id(1)

    @pl.when(j == 0)
    def _():
        acc_ref[...] = jnp.zeros_like(acc_ref)

    acc = acc_ref[...]
    width = w_ref.shape[1]
    own = _own_head(heads, width)
    w_all = w_ref[...]
    for p in range(pages_per_step):
        vp = v_refs[p][...]
        vp = vp.reshape(vp.shape[0] * heads, vp.shape[2]).astype(BF16)
        wm = jnp.where(own, jnp.broadcast_to(w_all[p:p + 1, :], (heads, width)), 0.0).astype(BF16)
        acc = acc + _dot(wm, vp)
    acc_ref[...] = acc

    @pl.when(j == pl.num_programs(1) - 1)
    def _():
        o_ref[...] = acc.astype(o_ref.dtype)


def _sb_values(w, cache_v, page_table, *, pages_per_step):
    db, n_pages = page_table.shape
    _, _, page, heads, hd = cache_v.shape
    steps = n_pages // pages_per_step
    out = pl.pallas_call(
        functools.partial(_sb_values_kernel, pages_per_step=pages_per_step, heads=heads),
        grid_spec=pltpu.PrefetchScalarGridSpec(
            num_scalar_prefetch=1,
            grid=(db, steps),
            in_specs=[pl.BlockSpec((None, pages_per_step, page * heads), lambda b, j, pt: (b, j, 0))]
            + [_page_spec(p, n_pages, pages_per_step, page, heads, hd) for p in range(pages_per_step)],
            out_specs=pl.BlockSpec((None, heads, hd), lambda b, j, pt: (b, 0, 0)),
            scratch_shapes=[pltpu.VMEM((heads, hd), F32)],
        ),
        out_shape=jax.ShapeDtypeStruct((db, heads, hd), BF16),
        compiler_params=_cparams(("parallel", "arbitrary")),
        name="sb_sample_values",
    )(page_table.reshape(-1), w, *([cache_v] * pages_per_step))
    return out.reshape(db, heads * hd)


def _ssd_prompt_kernel(x_ref, bm_ref, cm_ref, z_ref, dtc_ref, dtr_ref,
                       wx_ref, wb_ref, wc_ref, bx_ref, bb_ref, bc_ref,
                       dtbc_ref, dtbr_ref, alc_ref, alr_ref, dskip_ref, gssm_ref,
                       tril_ref, triu_ref,
                       o_ref, st_ref,
                       ext_ref, ht_ref, *, chunk, hpg, pdim, nstate, conv_w):
    c = pl.program_id(2)
    gw = hpg * pdim
    halo = SUBLANES
    widths = (gw, nstate, nstate)
    offs = (0, gw, gw + nstate)

    @pl.when(c == 0)
    def _():
        ext_ref[0:halo, :] = jnp.zeros((halo, ext_ref.shape[1]), F32)
        ht_ref[...] = jnp.zeros_like(ht_ref)

    raws = (x_ref, bm_ref, cm_ref)
    wrefs = (wx_ref, wb_ref, wc_ref)
    brefs = (bx_ref, bb_ref, bc_ref)
    conv = []
    for raw, wref, bref, off, wd in zip(raws, wrefs, brefs, offs, widths):
        ext_ref[halo:halo + chunk, off:off + wd] = raw[...]
        acc = jnp.broadcast_to(bref[...], (chunk, wd))
        for j in range(conv_w):
            start = halo - (conv_w - 1) + j
            acc = acc + ext_ref[start:start + chunk, off:off + wd] * wref[j:j + 1, :]
        conv.append(_silu(acc))
    ext_ref[0:halo, :] = ext_ref[chunk:chunk + halo, :]
    xs, bmat, cmat = conv

    dt_c = _softplus(dtc_ref[...] + dtbc_ref[...])
    dt_r = _softplus(dtr_ref[...] + dtbr_ref[...])
    a_c = -jnp.exp(alc_ref[...])
    a_r = -jnp.exp(alr_ref[...])
    da_c = _dot3_r(tril_ref[...], dt_c * a_c)
    da_r = _dot3(dt_r * a_r, triu_ref[...])

    bmat_b = bmat.astype(BF16)
    cmat_b = cmat.astype(BF16)
    cb = _dot_nt(cmat_b, bmat_b)
    bmat_t = bmat.T.astype(BF16)

    row = lax.broadcasted_iota(jnp.int32, (chunk, chunk), 0)
    col = lax.broadcasted_iota(jnp.int32, (chunk, chunk), 1)
    causal = col <= row
    first = lax.broadcasted_iota(jnp.int32, (chunk, LANES), 1) < pdim

    heads_per_tile = LANES // pdim
    assert heads_per_tile == 2
    us = []
    ssq = jnp.zeros((chunk, 1), F32)
    for t in range(hpg // heads_per_tile):
        r0 = t * heads_per_tile
        lanes = slice(t * LANES, (t + 1) * LANES)
        da_bc = [jnp.broadcast_to(da_c[:, r:r + 1], (chunk, LANES)) for r in (r0, r0 + 1)]
        dt_bc = [jnp.broadcast_to(dt_c[:, r:r + 1], (chunk, LANES)) for r in (r0, r0 + 1)]
        da_pair = jnp.where(first, da_bc[0], da_bc[1])
        dt_pair = jnp.where(first, dt_bc[0], dt_bc[1])
        last_pair = da_pair[chunk - 1:chunk, :]
        decay = []
        for i, r in enumerate((r0, r0 + 1)):
            diff = da_bc[i] - da_r[r:r + 1, :]
            decay.append((cb * jnp.where(causal, jnp.exp(jnp.where(causal, diff, 0.0)), 0.0)).astype(BF16))
        m_cat = jnp.concatenate(decay, axis=1)
        xp = xs[:, lanes]
        xdt = xp * dt_pair
        xbd = jnp.concatenate([jnp.where(first, xdt, 0.0), jnp.where(first, 0.0, xdt)], axis=0).astype(BF16)
        y = _dot(m_cat, xbd)
        ht_old = ht_ref[:, lanes]
        y = y + _dot(cmat_b, ht_old.astype(BF16)) * jnp.exp(da_pair)
        xw = (xdt * jnp.exp(last_pair - da_pair)).astype(BF16)
        ht_ref[:, lanes] = ht_old * jnp.exp(last_pair) + _dot(bmat_t, xw)
        y = y + xp * dskip_ref[:, lanes]
        u = y * _silu(z_ref[:, lanes])
        ssq = ssq + jnp.sum(u * u, axis=1, keepdims=True)
        us.append(u)
    inv = lax.rsqrt(ssq / gw + EPS)
    for t, u in enumerate(us):
        lanes = slice(t * LANES, (t + 1) * LANES)
        o_ref[:, lanes] = (u * inv * gssm_ref[:, lanes]).astype(o_ref.dtype)

    @pl.when(c == pl.num_programs(2) - 1)
    def _():
        st_ref[...] = ht_ref[...]


def _ssd_prompt(zx, dt_raw, conv_w, conv_b, dt_bias, a_log, d_skip, g_ssm, *, batch, seq, dims):
    heads, pdim, nstate, groups = dims
    hpg = heads // groups
    gw = hpg * pdim
    inner = heads * pdim
    chunk = min(128, seq)
    n_chunks = seq // chunk
    cw = conv_w.shape[0]
    zx3 = zx.reshape(batch, seq, zx.shape[1])
    dt4 = dt_raw.reshape(batch, seq, groups, hpg)
    dtc = dt4.transpose(0, 2, 1, 3)
    dtr = dt4.transpose(0, 2, 3, 1)
    idx = jnp.arange(chunk)
    tril = (idx[:, None] >= idx[None, :]).astype(BF16)
    triu = (idx[:, None] <= idx[None, :]).astype(BF16)

    zcol0 = 0
    xcol0 = inner // gw
    bcol0 = (2 * inner) // nstate
    ccol0 = (2 * inner + groups * nstate) // nstate
    wb0 = inner // nstate
    wc0 = (inner + groups * nstate) // nstate

    def tok(width, col0):
        return pl.BlockSpec((None, chunk, width), lambda b, g, c: (b, c, col0 + g))

    def par(rows, width, col0):
        return pl.BlockSpec((rows, width), lambda b, g, c: (0, col0 + g))

    def grp(shape):
        return pl.BlockSpec((None,) + shape, lambda b, g, c: (g, 0, 0))

    full = pl.BlockSpec((chunk, chunk), lambda b, g, c: (0, 0))
    conv_b2 = conv_b.reshape(1, -1)
    o_b, st = pl.pallas_call(
        functools.partial(_ssd_prompt_kernel, chunk=chunk, hpg=hpg, pdim=pdim, nstate=nstate, conv_w=cw),
        grid=(batch, groups, n_chunks),
        in_specs=[
            tok(gw, xcol0), tok(nstate, bcol0), tok(nstate, ccol0), tok(gw, zcol0),
            pl.BlockSpec((None, None, chunk, hpg), lambda b, g, c: (b, g, c, 0)),
            pl.BlockSpec((None, None, hpg, chunk), lambda b, g, c: (b, g, 0, c)),
            par(cw, gw, 0), par(cw, nstate, wb0), par(cw, nstate, wc0),
            par(1, gw, 0), par(1, nstate, wb0), par(1, nstate, wc0),
            grp((1, hpg)), grp((hpg, 1)), grp((1, hpg)), grp((hpg, 1)),
            par(1, gw, 0), par(1, gw, 0),
            full, full,
        ],
        out_specs=[
            pl.BlockSpec((None, chunk, gw), lambda b, g, c: (b, c, g)),
            pl.BlockSpec((None, None, nstate, gw), lambda b, g, c: (b, g, 0, 0)),
        ],
        out_shape=[
            jax.ShapeDtypeStruct((batch, seq, inner), BF16),
            jax.ShapeDtypeStruct((batch, groups, nstate, gw), F32),
        ],
        scratch_shapes=[
            pltpu.VMEM((chunk + SUBLANES, gw + 2 * nstate), F32),
            pltpu.VMEM((nstate, gw), F32),
        ],
        compiler_params=_cparams(("parallel", "parallel", "arbitrary")),
        name="ssd_prompt",
    )(zx3, zx3, zx3, zx3, dtc, dtr,
      conv_w, conv_w, conv_w, conv_b2, conv_b2, conv_b2,
      dt_bias.reshape(groups, 1, hpg), dt_bias.reshape(groups, hpg, 1),
      a_log.reshape(groups, 1, hpg), a_log.reshape(groups, hpg, 1),
      jnp.repeat(d_skip, pdim).reshape(1, inner), g_ssm.reshape(1, inner),
      tril, triu)
    st = st.reshape(batch, groups, nstate, hpg, pdim).transpose(0, 1, 3, 4, 2)
    return o_b.reshape(batch * seq, inner), st.reshape(batch, heads, pdim, nstate)


def _ssd_sample_kernel(xbc_ref, z_ref, dt_ref, cst_ref, h0_ref,
                       cw_ref, cb_ref, dtb_ref, al_ref, dskip_ref, gssm_ref, expand_ref,
                       o_ref, cnew_ref, h_ref, *, inner, nstate, groups, conv_w):
    gw = inner // groups
    x = xbc_ref[...]
    acc = cb_ref[...] + x * cw_ref[conv_w - 1:conv_w, :]
    for j in range(conv_w - 1):
        acc = acc + cst_ref[j:j + 1, :] * cw_ref[j:j + 1, :]
        if j > 0:
            cnew_ref[j - 1:j, :] = cst_ref[j:j + 1, :]
    cnew_ref[conv_w - 2:conv_w - 1, :] = x
    xc = _silu(acc)
    xs = xc[:, :inner]
    bv = xc[:, inner:inner + groups * nstate]
    cv = xc[:, inner + groups * nstate:]

    dt8 = jnp.broadcast_to(dt_ref[...], (SUBLANES, dt_ref.shape[1]))
    dt = _softplus(_dot3(dt8, expand_ref[...])[0:1, :] + dtb_ref[...])
    dec = jnp.exp(dt * (-jnp.exp(al_ref[...])))
    xdt = xs * dt
    row8 = lax.broadcasted_iota(jnp.int32, (SUBLANES, gw), 0)
    ys = []
    for g in range(groups):
        lanes = slice(g * gw, (g + 1) * gw)
        glanes = slice(g * nstate, (g + 1) * nstate)
        stacked = jnp.where(row8 == 0, xdt[:, lanes], jnp.where(row8 == 1, dec[:, lanes], 0.0))
        cols = stacked.T
        hn = h0_ref[lanes, :] * cols[:, 1:2] + cols[:, 0:1] * bv[:, glanes]
        h_ref[lanes, :] = hn
        cg = jnp.broadcast_to(cv[:, glanes], (SUBLANES, nstate)).astype(BF16)
        ys.append(_dot_nt(cg, hn.astype(BF16))[0:1, :])
    y = jnp.concatenate(ys, axis=1) + xs * dskip_ref[...]
    u = y * _silu(z_ref[...])
    for g in range(groups):
        lanes = slice(g * gw, (g + 1) * gw)
        ug = u[:, lanes]
        inv = lax.rsqrt(jnp.mean(ug * ug, axis=1, keepdims=True) + EPS)
        o_ref[:, lanes] = (ug * inv * gssm_ref[:, lanes]).astype(o_ref.dtype)


def _ssd_sample(xbc, z, dt_raw, state_conv, state_ssm, conv_w, conv_b, dt_bias, a_log, d_skip, g_ssm, *, dims):
    heads, pdim, nstate, groups = dims
    inner = heads * pdim
    db, conv_dim = xbc.shape
    cw = conv_w.shape[0]
    rep = lambda v: jnp.repeat(v.astype(F32), pdim).reshape(1, inner)
    expand = (jnp.arange(heads)[:, None] == (jnp.arange(inner)[None, :] // pdim)).astype(BF16)
    row = lambda width: pl.BlockSpec((None, 1, width), lambda b: (b, 0, 0))
    full = lambda shape: pl.BlockSpec(shape, lambda b: (0,) * len(shape))
    o_b, conv_new, h_new = pl.pallas_call(
        functools.partial(_ssd_sample_kernel, inner=inner, nstate=nstate, groups=groups, conv_w=cw),
        grid=(db,),
        in_specs=[
            row(conv_dim), row(inner), row(heads),
            pl.BlockSpec((None, cw - 1, conv_dim), lambda b: (b, 0, 0)),
            pl.BlockSpec((None, inner, nstate), lambda b: (b, 0, 0)),
            full((cw, conv_dim)), full((1, conv_dim)),
            full((1, inner)), full((1, inner)), full((1, inner)), full((1, inner)),
            full((heads, inner)),
        ],
        out_specs=[
            row(inner),
            pl.BlockSpec((None, cw - 1, conv_dim), lambda b: (b, 0, 0)),
            pl.BlockSpec((None, inner, nstate), lambda b: (b, 0, 0)),
        ],
        out_shape=[
            jax.ShapeDtypeStruct((db, 1, inner), BF16),
            jax.ShapeDtypeStruct((db, cw - 1, conv_dim), F32),
            jax.ShapeDtypeStruct((db, inner, nstate), F32),
        ],
        compiler_params=_cparams(("parallel",)),
        name="ssd_sample",
    )(xbc.reshape(db, 1, conv_dim), z.reshape(db, 1, inner), dt_raw.reshape(db, 1, heads),
      state_conv, state_ssm.reshape(db, inner, nstate),
      conv_w, conv_b.reshape(1, conv_dim), rep(dt_bias), rep(a_log), rep(d_skip), g_ssm.reshape(1, inner),
      expand)
    return o_b.reshape(db, inner), conv_new, h_new.reshape(db, heads, pdim, nstate)


def _merge_kernel(oa_ref, ob_ref, wa_ref, wb_ref, ga_ref, gb_ref, o_ref):
    a = _dot(oa_ref[...], wa_ref[...])
    b = _dot(ob_ref[...], wb_ref[...])
    o_ref[...] = (jax.nn.sigmoid(ga_ref[...]) * a + jax.nn.sigmoid(gb_ref[...]) * b).astype(o_ref.dtype)


def _merge(o_a, o_b, wa, wb, gates, *, off_ga, off_gb):
    m, ka = o_a.shape
    kb = o_b.shape[1]
    n = wa.shape[1]
    tm = min(ROW_TILE, m)
    tn = COL_TILE
    ga0, gb0 = off_ga // tn, off_gb // tn
    return pl.pallas_call(
        _merge_kernel,
        grid=(m // tm, n // tn),
        in_specs=[
            pl.BlockSpec((tm, ka), lambda i, j: (i, 0)),
            pl.BlockSpec((tm, kb), lambda i, j: (i, 0)),
            pl.BlockSpec((ka, tn), lambda i, j: (0, j)),
            pl.BlockSpec((kb, tn), lambda i, j: (0, j)),
            pl.BlockSpec((tm, tn), lambda i, j: (i, ga0 + j)),
            pl.BlockSpec((tm, tn), lambda i, j: (i, gb0 + j)),
        ],
        out_specs=pl.BlockSpec((tm, tn), lambda i, j: (i, j)),
        out_shape=jax.ShapeDtypeStruct((m, n), BF16),
        compiler_params=_cparams(("parallel", "arbitrary")),
        name="merge",
    )(o_a, o_b, wa, wb, gates, gates)


def _resid_mm_kernel(x_ref, a_ref, w_ref, g_ref, o_ref, on_ref):
    y = x_ref[...] + _dot(a_ref[...], w_ref[...])
    o_ref[...] = y
    on_ref[...] = _rms_rows(y, g_ref[...]).astype(BF16)


def _resid_matmul_norm(x, a, w, g):
    m, n = x.shape
    k = a.shape[1]
    tm = min(256, m)
    return pl.pallas_call(
        _resid_mm_kernel,
        grid=(m // tm,),
        in_specs=[
            pl.BlockSpec((tm, n), lambda i: (i, 0)),
            pl.BlockSpec((tm, k), lambda i: (i, 0)),
            pl.BlockSpec((k, n), lambda i: (0, 0)),
            pl.BlockSpec((1, n), lambda i: (0, 0)),
        ],
        out_specs=[pl.BlockSpec((tm, n), lambda i: (i, 0)), pl.BlockSpec((tm, n), lambda i: (i, 0))],
        out_shape=[jax.ShapeDtypeStruct((m, n), F32), jax.ShapeDtypeStruct((m, n), BF16)],
        compiler_params=_cparams(("parallel",)),
        name="resid_matmul_norm",
    )(x, a, w, g.reshape(1, n))


def _resid_mm_tiled_kernel(x_ref, a_ref, w_ref, o_ref):
    o_ref[...] = x_ref[...] + _dot(a_ref[...], w_ref[...])


def _resid_matmul(x, a, w):
    m, n = x.shape
    k = a.shape[1]
    tm = min(ROW_TILE, m)
    tn = COL_TILE
    return pl.pallas_call(
        _resid_mm_tiled_kernel,
        grid=(m // tm, n // tn),
        in_specs=[
            pl.BlockSpec((tm, tn), lambda i, j: (i, j)),
            pl.BlockSpec((tm, k), lambda i, j: (i, 0)),
            pl.BlockSpec((k, tn), lambda i, j: (0, j)),
        ],
        out_specs=pl.BlockSpec((tm, tn), lambda i, j: (i, j)),
        out_shape=jax.ShapeDtypeStruct((m, n), F32),
        compiler_params=_cparams(("parallel", "arbitrary")),
        name="resid_matmul",
    )(x, a, w)


def _ffn_up_prompt_kernel(xn_ref, wa_ref, wb_ref, cwa_ref, cwb_ref, cba_ref, cbb_ref,
                          act_ref, ta_ref, tb_ref, exta_ref, extb_ref, wba_ref, wbb_ref, *, seq, conv_w):
    i = pl.program_id(1)
    tm = xn_ref.shape[0]
    halo = SUBLANES

    @pl.when(i == 0)
    def _():
        wba_ref[...] = wa_ref[...].astype(BF16)
        wbb_ref[...] = wb_ref[...].astype(BF16)

    xn = xn_ref[...]
    start_of_seq = (i * tm) % seq == 0
    outs = []
    for w_ref, cw_ref, cb_ref, ext_ref, tail_ref in ((wba_ref, cwa_ref, cba_ref, exta_ref, ta_ref),
                                                     (wbb_ref, cwb_ref, cbb_ref, extb_ref, tb_ref)):
        up = _dot(xn, w_ref[...])

        @pl.when(start_of_seq)
        def _():
            ext_ref[0:halo, :] = jnp.zeros((halo, ext_ref.shape[1]), F32)

        ext_ref[halo:halo + tm, :] = up
        acc = jnp.broadcast_to(cb_ref[...], up.shape)
        for j in range(conv_w):
            start = halo - (conv_w - 1) + j
            acc = acc + ext_ref[start:start + tm, :] * cw_ref[j:j + 1, :]
        tail = ext_ref[tm:tm + halo, :]
        ext_ref[0:halo, :] = tail
        tail_ref[...] = tail
        outs.append(acc)
    act_ref[...] = (_silu(outs[0]) * outs[1]).astype(act_ref.dtype)


def _ffn_up_prompt(xn, w_up, cw, cb, *, seq):
    m, k = xn.shape
    n2 = w_up.shape[1]
    dff = n2 // 2
    tm = min(ROW_TILE, m, seq)
    tn = COL_TILE
    nj = dff // tn
    cwid = cw.shape[0]
    cb2 = cb.reshape(1, n2)
    act, ta, tb = pl.pallas_call(
        functools.partial(_ffn_up_prompt_kernel, seq=seq, conv_w=cwid),
        grid=(nj, m // tm),
        in_specs=[
            pl.BlockSpec((tm, k), lambda j, i: (i, 0)),
            pl.BlockSpec((k, tn), lambda j, i: (0, j)),
            pl.BlockSpec((k, tn), lambda j, i: (0, nj + j)),
            pl.BlockSpec((cwid, tn), lambda j, i: (0, j)),
            pl.BlockSpec((cwid, tn), lambda j, i: (0, nj + j)),
            pl.BlockSpec((1, tn), lambda j, i: (0, j)),
            pl.BlockSpec((1, tn), lambda j, i: (0, nj + j)),
        ],
        out_specs=[
            pl.BlockSpec((tm, tn), lambda j, i: (i, j)),
            pl.BlockSpec((None, SUBLANES, tn), lambda j, i: (i, 0, j)),
            pl.BlockSpec((None, SUBLANES, tn), lambda j, i: (i, 0, j)),
        ],
        out_shape=[
            jax.ShapeDtypeStruct((m, dff), BF16),
            jax.ShapeDtypeStruct((m // tm, SUBLANES, dff), F32),
            jax.ShapeDtypeStruct((m // tm, SUBLANES, dff), F32),
        ],
        scratch_shapes=[pltpu.VMEM((tm + SUBLANES, tn), F32), pltpu.VMEM((tm + SUBLANES, tn), F32),
                        pltpu.VMEM((k, tn), BF16), pltpu.VMEM((k, tn), BF16)],
        compiler_params=_cparams(("parallel", "arbitrary")),
        name="ffn_up_prompt",
    )(xn, w_up, w_up, cw, cw, cb2, cb2)
    per_seq = seq // tm
    tails = jnp.concatenate([ta, tb], axis=-1)[per_seq - 1::per_seq, SUBLANES - (cwid - 1):, :]
    return act, tails


def _ffn_up_sample_kernel(xn_ref, wa_ref, wb_ref, cwa_ref, cwb_ref, cba_ref, cbb_ref, sta_ref, stb_ref,
                          act_ref, ua_ref, ub_ref, *, conv_w):
    xn = xn_ref[...]
    outs = []
    for w_ref, cw_ref, cb_ref, st_ref, u_ref in ((wa_ref, cwa_ref, cba_ref, sta_ref, ua_ref),
                                                 (wb_ref, cwb_ref, cbb_ref, stb_ref, ub_ref)):
        up = _dot(xn, w_ref[...].astype(BF16))
        u_ref[...] = up
        acc = cb_ref[...] + up * cw_ref[conv_w - 1:conv_w, :]
        for j in range(conv_w - 1):
            acc = acc + st_ref[j] * cw_ref[j:j + 1, :]
        outs.append(acc)
    act_ref[...] = (_silu(outs[0]) * outs[1]).astype(act_ref.dtype)


def _ffn_up_sample(xn, w_up, cw, cb, state):
    m, k = xn.shape
    n2 = w_up.shape[1]
    dff = n2 // 2
    tn = COL_TILE
    nj = dff // tn
    cwid = cw.shape[0]
    cb2 = cb.reshape(1, n2)
    st = state.transpose(1, 0, 2)
    act, ua, ub = pl.pallas_call(
        functools.partial(_ffn_up_sample_kernel, conv_w=cwid),
        grid=(nj,),
        in_specs=[
            pl.BlockSpec((m, k), lambda j: (0, 0)),
            pl.BlockSpec((k, tn), lambda j: (0, j)),
            pl.BlockSpec((k, tn), lambda j: (0, nj + j)),
            pl.BlockSpec((cwid, tn), lambda j: (0, j)),
            pl.BlockSpec((cwid, tn), lambda j: (0, nj + j)),
            pl.BlockSpec((1, tn), lambda j: (0, j)),
            pl.BlockSpec((1, tn), lambda j: (0, nj + j)),
            pl.BlockSpec((cwid - 1, m, tn), lambda j: (0, 0, j)),
            pl.BlockSpec((cwid - 1, m, tn), lambda j: (0, 0, nj + j)),
        ],
        out_specs=[pl.BlockSpec((m, tn), lambda j: (0, j))] * 3,
        out_shape=[
            jax.ShapeDtypeStruct((m, dff), BF16),
            jax.ShapeDtypeStruct((m, dff), F32),
            jax.ShapeDtypeStruct((m, dff), F32),
        ],
        compiler_params=_cparams(("parallel",)),
        name="ffn_up_sample",
    )(xn, w_up, w_up, cw, cw, cb2, cb2, st, st)
    up = jnp.concatenate([ua, ub], axis=-1)
    new_state = jnp.concatenate([state[:, 1:, :], up[:, None, :]], axis=1)
    return act, new_state


def _ple_kernel(x_ref, xt_ref, p_ref, g_ref, wp_ref, wg_ref, o_ref, xn_ref, *, row_chunk):
    @pl.when(pl.program_id(1) == 0)
    def _():
        tm = x_ref.shape[0]
        for c in range(tm // row_chunk):
            rows = pl.ds(c * row_chunk, row_chunk)
            xn_ref[rows, :] = _rms_rows(x_ref[rows, :], g_ref[...]).astype(BF16)

    emb = _dot(p_ref[...].astype(BF16), wp_ref[...])
    gate = _dot(xn_ref[...], wg_ref[...])
    o_ref[...] = xt_ref[...] + emb * jax.nn.sigmoid(gate)


def _ple(x, p, g, w_ple, w_gate):
    m, d = x.shape
    kp = p.shape[1]
    tm = min(ROW_TILE, m)
    tn = COL_TILE
    row_chunk = min(256, tm)
    return pl.pallas_call(
        functools.partial(_ple_kernel, row_chunk=row_chunk),
        grid=(m // tm, d // tn),
        in_specs=[
            pl.BlockSpec((tm, d), lambda i, j: (i, 0)),
            pl.BlockSpec((tm, tn), lambda i, j: (i, j)),
            pl.BlockSpec((tm, kp), lambda i, j: (i, 0)),
            pl.BlockSpec((1, d), lambda i, j: (0, 0)),
            pl.BlockSpec((kp, tn), lambda i, j: (0, j)),
            pl.BlockSpec((d, tn), lambda i, j: (0, j)),
        ],
        out_specs=pl.BlockSpec((tm, tn), lambda i, j: (i, j)),
        out_shape=jax.ShapeDtypeStruct((m, d), F32),
        scratch_shapes=[pltpu.VMEM((tm, d), BF16)],
        compiler_params=_cparams(("parallel", "arbitrary")),
        name="ple",
    )(x, x, p, g.reshape(1, d), w_ple, w_gate)


def kernel(x_prompt, x_sample, cache_k, cache_v, state_ssm, state_conv, state_ffn_conv, page_table,
           p_prompt, p_sample, g_mix, w_in, g_q, g_k, sb_bias, conv_w, conv_b, dt_bias, a_log, d_skip, g_ssm,
           w_proj_a, w_proj_b, w_out, g_ffn, w_up, ffn_conv_w, ffn_conv_b, w_down, w_ple, g_ple, w_ple_gate):
    depth = w_in.shape[0]
    assert depth == 1
    batch, seq, d_model = x_prompt.shape
    db, dseq, _ = x_sample.shape
    assert dseq == 1
    _, n_pool, page, sb_heads, hd = cache_k.shape
    sb_width = sb_heads * hd
    _, _, heads, pdim, nstate = state_ssm.shape
    inner = heads * pdim
    conv_dim = state_conv.shape[-1]
    groups = (conv_dim - inner) // (2 * nstate)
    dims = (heads, pdim, nstate, groups)

    off_z = 3 * sb_width
    off_dt = off_z + inner + conv_dim
    off_ga = off_dt + heads
    w_in0 = w_in[0]
    w_tail = jnp.concatenate(
        [w_in0[:, off_dt:off_ga].astype(BF16), jnp.zeros((d_model, COL_TILE - heads), BF16),
         w_in0[:, off_ga:].astype(BF16)], axis=1)
    tail_ga, tail_gb = COL_TILE, COL_TILE + d_model
    wa_b = w_proj_a[0].astype(BF16)
    wb_b = w_proj_b[0].astype(BF16)
    wout_b = w_out[0].astype(BF16)
    wdown_b = w_down[0].astype(BF16)
    wple_b = w_ple[0].astype(BF16)
    wgate_b = w_ple_gate[0].astype(BF16)
    gq = g_q[0].reshape(1, hd)
    gk = g_k[0].reshape(1, hd)

    def mixer_inputs(x2):
        xn = _rms_norm_bf16(x2, g_mix[0])
        proj = functools.partial(_proj, xn, w_in0, tn=PROJ_COL_TILE, hd=hd)
        (qn,) = proj(col0=0, ncols=sb_width, epilogue=functools.partial(_epi_q, hd=hd), extras=[gq],
                     out_kinds=["bf16"], name="proj_q")
        kn, knb = proj(col0=sb_width, ncols=sb_width, epilogue=functools.partial(_epi_k, hd=hd), extras=[gk],
                       out_kinds=["heads", "bf16"], name="proj_k")
        v, vbt = proj(col0=2 * sb_width, ncols=sb_width, epilogue=functools.partial(_epi_v, hd=hd), extras=[],
                      out_kinds=["heads", "bf16_t"], name="proj_v")
        (zx,) = proj(col0=off_z, ncols=inner + conv_dim, epilogue=_epi_plain, extras=[],
                     out_kinds=["f32"], name="proj_zx")
        (tail,) = _proj(xn, w_tail, col0=0, ncols=w_tail.shape[1], tn=COL_TILE, epilogue=_epi_plain, extras=[],
                        out_kinds=["f32"], hd=hd, name="proj_tail")
        return qn, kn, knb, v, vbt, zx, tail

    def channel_mix(x2, o_a, o_b, tail, p2, ffn_up):
        merged = _merge(o_a, o_b, wa_b, wb_b, tail, off_ga=tail_ga, off_gb=tail_gb)
        x1, xn1 = _resid_matmul_norm(x2, merged, wout_b, g_ffn[0])
        act, ffn_state = ffn_up(xn1)
        x2b = _resid_matmul(x1, act, wdown_b)
        x3 = _ple(x2b, p2, g_ple[0], wple_b, wgate_b)
        return x3, ffn_state

    xp2 = x_prompt.reshape(batch * seq, d_model)
    qn, kn, knb, v, vbt, zx, tail = mixer_inputs(xp2)
    o_a = _sb_prompt(qn, knb, vbt, sb_bias[0], batch=batch, seq=seq, heads=sb_heads, hd=hd)
    o_b, ssm_p = _ssd_prompt(zx, tail[:, :heads], conv_w[0], conv_b[0], dt_bias[0], a_log[0], d_skip[0], g_ssm[0],
                             batch=batch, seq=seq, dims=dims)
    cw = conv_w.shape[1]
    conv_p = zx.reshape(batch, seq, -1)[:, seq - (cw - 1):, inner:]
    yp, ffn_p = channel_mix(
        xp2, o_a, o_b, tail, p_prompt[0].reshape(batch * seq, -1),
        lambda xn1: _ffn_up_prompt(xn1, w_up[0], ffn_conv_w[0], ffn_conv_b[0], seq=seq))
    k_p = kn.reshape(batch, seq, sb_heads, hd)
    v_p = v.reshape(batch, seq, sb_heads, hd)

    xs2 = x_sample.reshape(db, d_model)
    qn_s, kn_s, _, v_s, _, zx_s, tail_s = mixer_inputs(xs2)
    n_pages = page_table.shape[1]
    pps = PAGES_PER_STEP if n_pages % PAGES_PER_STEP == 0 else 1
    z_s = _sb_scores(qn_s, cache_k, page_table, pages_per_step=pps)
    z_s = z_s.reshape(db, n_pages, page, sb_heads).transpose(0, 1, 3, 2)
    w_s = _sb_weights(z_s, sb_bias[0])
    w_s = w_s.transpose(0, 1, 3, 2).reshape(db, n_pages, page * sb_heads)
    o_a_s = _sb_values(w_s, cache_v, page_table, pages_per_step=pps)
    o_b_s, conv_s, ssm_s = _ssd_sample(
        zx_s[:, inner:], zx_s[:, :inner], tail_s[:, :heads], state_conv[0], state_ssm[0],
        conv_w[0], conv_b[0], dt_bias[0], a_log[0], d_skip[0], g_ssm[0], dims=dims)
    ys, ffn_s = channel_mix(
        xs2, o_a_s, o_b_s, tail_s, p_sample[0].reshape(db, -1),
        lambda xn1: _ffn_up_sample(xn1, w_up[0], ffn_conv_w[0], ffn_conv_b[0], state_ffn_conv[0]))

    return (yp.reshape(batch, seq, d_model), ys.reshape(db, 1, d_model),
            k_p[None], v_p[None], ssm_p[None], conv_p[None], ffn_p[None],
            kn_s.reshape(db, 1, sb_heads, hd)[None], v_s.reshape(db, 1, sb_heads, hd)[None],
            ssm_s[None], conv_s[None], ffn_s[None])
```

```python
import functools
import math

import jax
import jax.numpy as jnp
from jax import lax
from jax.experimental import pallas as pl
from jax.experimental.pallas import tpu as pltpu

F32 = jnp.float32
BF16 = jnp.bfloat16
EPS = 1e-6
LANES = 128
SUBLANES = 8
VMEM_LIMIT = 56 * 1024 * 1024
COL_TILE = 512
ROW_TILE = 1024
PROJ_COL_TILE = 1024
ATTN_BLOCK = 256
ATTN_HEADS_PER_STEP = 4
PAGES_PER_STEP = 16
FFN_ROW_TILE = 512


def _cparams(sem):
    return pltpu.CompilerParams(dimension_semantics=sem, vmem_limit_bytes=VMEM_LIMIT)


def _dot(a, b):
    return jnp.dot(a, b, preferred_element_type=F32)


def _dot_nt(a, b):
    return lax.dot_general(a, b, (((1,), (1,)), ((), ())), preferred_element_type=F32)


def _split3(x):
    hi = x.astype(BF16)
    r = x - hi.astype(F32)
    mid = r.astype(BF16)
    lo = (r - mid.astype(F32)).astype(BF16)
    return hi, mid, lo


def _dot3(x, m):
    hi, mid, lo = _split3(x)
    return _dot(hi, m) + _dot(mid, m) + _dot(lo, m)


def _dot3_r(m, x):
    hi, mid, lo = _split3(x)
    return _dot(m, hi) + _dot(m, mid) + _dot(m, lo)


def _softplus(x):
    return jnp.maximum(x, 0.0) + jnp.log(1.0 + jnp.exp(-jnp.abs(x)))


LOG2E = math.log2(math.e)


def _softplus2(x):
    neg_abs = lax.bitcast_convert_type(lax.bitcast_convert_type(x, jnp.uint32) | jnp.uint32(0x80000000), F32)
    return jnp.maximum(x, 0.0) + jnp.log2(1.0 + jnp.exp2(neg_abs))


def _silu(x):
    return x * jax.nn.sigmoid(x)


def _rms_rows(x, g):
    ms = jnp.mean(x * x, axis=-1, keepdims=True)
    return x * lax.rsqrt(ms + EPS) * g


def _rms_norm_kernel(x_ref, g_ref, o_ref):
    o_ref[...] = _rms_rows(x_ref[...], g_ref[...]).astype(BF16)


def _rms_norm_bf16(x, g):
    m, k = x.shape
    tm = min(256, m)
    return pl.pallas_call(
        _rms_norm_kernel,
        grid=(m // tm,),
        in_specs=[pl.BlockSpec((tm, k), lambda i: (i, 0)), pl.BlockSpec((1, k), lambda i: (0, 0))],
        out_specs=pl.BlockSpec((tm, k), lambda i: (i, 0)),
        out_shape=jax.ShapeDtypeStruct((m, k), BF16),
        compiler_params=_cparams(("parallel",)),
        name="rms_norm",
    )(x, g.reshape(1, k))


def _proj_kernel(x_ref, w_ref, *rest, n_extra, n_out, epilogue):
    extras = rest[:n_extra]
    outs = rest[n_extra:n_extra + n_out]
    wb_ref = rest[n_extra + n_out]

    @pl.when(pl.program_id(1) == 0)
    def _():
        wb_ref[...] = w_ref[...].astype(BF16)

    epilogue(_dot(x_ref[...], wb_ref[...]), *extras, *outs)


def _epi_plain(acc, o_ref):
    o_ref[...] = acc


def _epi_q(acc, g_ref, qn_ref, *, hd):
    scale = LOG2E / math.sqrt(hd)
    for h in range(acc.shape[1] // hd):
        lanes = slice(h * hd, (h + 1) * hd)
        qn_ref[:, lanes] = (_rms_rows(acc[:, lanes], g_ref[...]) * scale).astype(BF16)


def _epi_k(acc, g_ref, kn_ref, knb_ref, *, hd):
    for h in range(acc.shape[1] // hd):
        lanes = slice(h * hd, (h + 1) * hd)
        kn = _rms_rows(acc[:, lanes], g_ref[...])
        kn_ref[:, h, :] = kn
        knb_ref[:, lanes] = kn.astype(BF16)


def _epi_v(acc, v_ref, vbt_ref, *, hd):
    for h in range(acc.shape[1] // hd):
        lanes = slice(h * hd, (h + 1) * hd)
        v_ref[:, h, :] = acc[:, lanes]
        vbt_ref[lanes, :] = acc[:, lanes].T.astype(BF16)


def _proj(xn, w, *, col0, ncols, tn, epilogue, extras, out_kinds, hd, name):
    m, k = xn.shape
    tm = min(ROW_TILE, m)
    tn = min(tn, ncols)
    assert col0 % tn == 0 and ncols % tn == 0
    c0 = col0 // tn
    out_specs, out_shapes = [], []
    for kind in out_kinds:
        if kind == "heads":
            out_specs.append(pl.BlockSpec((tm, tn // hd, hd), lambda j, i: (i, j, 0)))
            out_shapes.append(jax.ShapeDtypeStruct((m, ncols // hd, hd), F32))
        elif kind == "bf16_t":
            out_specs.append(pl.BlockSpec((tn, tm), lambda j, i: (j, i)))
            out_shapes.append(jax.ShapeDtypeStruct((ncols, m), BF16))
        else:
            out_specs.append(pl.BlockSpec((tm, tn), lambda j, i: (i, j)))
            out_shapes.append(jax.ShapeDtypeStruct((m, ncols), F32 if kind == "f32" else BF16))
    extra_specs = [pl.BlockSpec(e.shape, lambda j, i: (0, 0)) for e in extras]
    return pl.pallas_call(
        functools.partial(_proj_kernel, n_extra=len(extras), n_out=len(out_kinds), epilogue=epilogue),
        grid=(ncols // tn, m // tm),
        in_specs=[pl.BlockSpec((tm, k), lambda j, i: (i, 0)),
                  pl.BlockSpec((k, tn), lambda j, i: (0, c0 + j))] + extra_specs,
        out_specs=out_specs,
        out_shape=out_shapes,
        scratch_shapes=[pltpu.VMEM((k, tn), BF16)],
        compiler_params=_cparams(("parallel", "arbitrary")),
        name=name,
    )(xn, w, *extras)


def _sb_prompt_kernel(bias_ref, q_ref, k_ref, vt_ref, tri_ref, o_ref, *, tq, tk, hd, hps):
    hb = pl.program_id(1)
    qi = pl.program_id(2)
    tri = tri_ref[...]

    def block(k0, state, masked):
        if masked:
            row = lax.broadcasted_iota(jnp.int32, (tq, tk), 0)
            col = lax.broadcasted_iota(jnp.int32, (tq, tk), 1)
            keep = col < row
        heads = range(hps)
        lanes = [slice(g * hd, (g + 1) * hd) for g in heads]
        zs = [_dot_nt(q_ref[:, lanes[g]], k_ref[pl.ds(k0, tk), lanes[g]]) + bias_ref[hb * hps + g] for g in heads]
        sps = [_softplus2(z) for z in zs]
        if masked:
            sps = [jnp.where(keep, sp, 0.0) for sp in sps]
        spb = [sp.astype(BF16) for sp in sps]
        tails = [_dot(sb, tri) for sb in spb]
        ws = [jnp.exp2(zs[g] - sps[g] - tails[g] - state[g][0]) for g in heads]
        if masked:
            ws = [jnp.where(keep, w, 0.0) for w in ws]
        pvs = [_dot_nt(vt_ref[lanes[g], pl.ds(k0, tk)], ws[g].astype(BF16)) for g in heads]
        sums = [tails[g][:, 0:1] + spb[g][:, 0:1].astype(F32) for g in heads]
        return tuple((state[g][0] + sums[g], state[g][1] + pvs[g]) for g in heads)

    state = tuple((jnp.zeros((tq, 1), F32), jnp.zeros((hd, tq), F32)) for _ in range(hps))
    state = block(pl.multiple_of(qi * tq, tq), state, True)

    def body(it, st):
        return block(pl.multiple_of((qi - 1 - it) * tk, tk), st, False)

    state = lax.fori_loop(0, qi, body, state)
    for g in range(hps):
        o_ref[:, g * hd:(g + 1) * hd] = state[g][1].T.astype(o_ref.dtype)


def _sb_prompt(qn, knb, vbt, sb_bias, *, batch, seq, heads, hd):
    tq = tk = min(ATTN_BLOCK, seq)
    hps = ATTN_HEADS_PER_STEP
    assert seq % tq == 0 and heads % hps == 0
    width = heads * hd
    q3 = qn.reshape(batch, seq, width)
    k3 = knb.reshape(batch, seq, width)
    tri = (jnp.arange(tk)[:, None] > jnp.arange(tk)[None, :]).astype(BF16)
    out = pl.pallas_call(
        functools.partial(_sb_prompt_kernel, tq=tq, tk=tk, hd=hd, hps=hps),
        grid=(batch, heads // hps, seq // tq),
        in_specs=[
            pl.BlockSpec(memory_space=pltpu.SMEM),
            pl.BlockSpec((None, tq, hps * hd), lambda b, h, i: (b, i, h)),
            pl.BlockSpec((None, seq, hps * hd), lambda b, h, i: (b, 0, h)),
            pl.BlockSpec((hps * hd, seq), lambda b, h, i: (h, b)),
            pl.BlockSpec((tk, tk), lambda b, h, i: (0, 0)),
        ],
        out_specs=pl.BlockSpec((None, tq, hps * hd), lambda b, h, i: (b, i, h)),
        out_shape=jax.ShapeDtypeStruct((batch, seq, width), BF16),
        compiler_params=_cparams(("parallel", "parallel", "arbitrary")),
        name="sb_attn_prompt",
    )(sb_bias.astype(F32) * LOG2E, q3, k3, vbt, tri)
    return out.reshape(batch * seq, width)


def _page_spec(p, n_pages, pages_per_step, page, heads, hd):
    return pl.BlockSpec((None, None, page, heads, hd),
                        lambda b, j, pt: (0, pt[b * n_pages + j * pages_per_step + p], 0, 0, 0))


def _own_head(heads, width):
    row = lax.broadcasted_iota(jnp.int32, (heads, width), 0)
    col = lax.broadcasted_iota(jnp.int32, (heads, width), 1)
    return col % heads == row


def _sb_weights_kernel(z_ref, bias_ref, tri_ref, ones_ref, mpage_ref, w_ref):
    n_pages, heads, page = z_ref.shape
    z = z_ref[...] + bias_ref[...]
    sp = _softplus2(z)
    log_1mb = (-sp).reshape(n_pages * heads, page)
    log_b = (z - sp).reshape(n_pages * heads, page)
    within = _dot3(log_1mb, tri_ref[...])
    totals = _dot3(log_1mb, ones_ref[...])
    later = _dot3_r(mpage_ref[...], totals)
    w = jnp.exp2(log_b + within + later)
    w_ref[...] = w.reshape(n_pages, heads, page)


def _sb_weights(z, sb_bias):
    db, n_pages, heads, page = z.shape
    rows = n_pages * heads
    tri = (jnp.arange(page)[:, None] > jnp.arange(page)[None, :]).astype(BF16)
    ones = jnp.ones((page, page), BF16)
    r = jnp.arange(rows)
    mpage = ((r[None, :] // heads > r[:, None] // heads) & (r[None, :] % heads == r[:, None] % heads)).astype(BF16)
    full = lambda shape: pl.BlockSpec(shape, lambda b: (0,) * len(shape))
    return pl.pallas_call(
        _sb_weights_kernel,
        grid=(db,),
        in_specs=[
            pl.BlockSpec((None, n_pages, heads, page), lambda b: (b, 0, 0, 0)),
            full((1, heads, 1)),
            full((page, page)),
            full((page, page)),
            full((rows, rows)),
        ],
        out_specs=pl.BlockSpec((None, n_pages, heads, page), lambda b: (b, 0, 0, 0)),
        out_shape=jax.ShapeDtypeStruct((db, n_pages, heads, page), F32),
        compiler_params=_cparams(("parallel",)),
        name="sb_sample_weights",
    )(z, (sb_bias.astype(F32) * LOG2E).reshape(1, heads, 1), tri, ones, mpage)


def _sb_values_kernel(pt_ref, w_ref, *rest, pages_per_step, heads):
    v_refs = rest[:pages_per_step]
    o_ref = rest[pages_per_step]
    acc_ref = rest[pages_per_step + 1]
    j = pl.program_id(1)

    @pl.when(j == 0)
    def _():
        acc_ref[...] = jnp.zeros_like(acc_ref)

    acc = acc_ref[...]
    width = w_ref.shape[1]
    own = _own_head(heads, width)
    w_all = w_ref[...]
    for p in range(pages_per_step):
        vp = v_refs[p][...]
        vp = vp.reshape(vp.shape[0] * heads, vp.shape[2]).astype(BF16)
        wm = jnp.where(own, jnp.broadcast_to(w_all[p:p + 1, :], (heads, width)), 0.0).astype(BF16)
        acc = acc + _dot(wm, vp)
    acc_ref[...] = acc

    @pl.when(j == pl.num_programs(1) - 1)
    def _():
        o_ref[...] = acc.astype(o_ref.dtype)


def _sb_values(w, cache_v, page_table, *, pages_per_step):
    db, n_pages = page_table.shape
    _, _, page, heads, hd = cache_v.shape
    steps = n_pages // pages_per_step
    out = pl.pallas_call(
        functools.partial(_sb_values_kernel, pages_per_step=pages_per_step, heads=heads),
        grid_spec=pltpu.PrefetchScalarGridSpec(
            num_scalar_prefetch=1,
            grid=(db, steps),
            in_specs=[pl.BlockSpec((None, pages_per_step, page * heads), lambda b, j, pt: (b, j, 0))]
            + [_page_spec(p, n_pages, pages_per_step, page, heads, hd) for p in range(pages_per_step)],
            out_specs=pl.BlockSpec((None, heads, hd), lambda b, j, pt: (b, 0, 0)),
            scratch_shapes=[pltpu.VMEM((heads, hd), F32)],
        ),
        out_shape=jax.ShapeDtypeStruct((db, heads, hd), BF16),
        compiler_params=_cparams(("parallel", "arbitrary")),
        name="sb_sample_values",
    )(page_table.reshape(-1), w, *([cache_v] * pages_per_step))
    return out.reshape(db, heads * hd)


def _ssd_prompt_kernel(x_ref, bm_ref, cm_ref, z_ref, dt_ref,
                       wx_ref, wb_ref, wc_ref, bx_ref, bb_ref, bc_ref,
                       dtb_ref, al_ref, dskip_ref, gssm_ref, triu_ref,
                       o_ref, st_ref,
                       ext_ref, ht_ref, *, chunk, hpg, pdim, nstate, conv_w):
    c = pl.program_id(2)
    gw = hpg * pdim
    halo = SUBLANES
    widths = (gw, nstate, nstate)
    offs = (0, gw, gw + nstate)

    @pl.when(c == 0)
    def _():
        ext_ref[0:halo, :] = jnp.zeros((halo, ext_ref.shape[1]), F32)
        ht_ref[...] = jnp.zeros_like(ht_ref)

    raws = (x_ref, bm_ref, cm_ref)
    wrefs = (wx_ref, wb_ref, wc_ref)
    brefs = (bx_ref, bb_ref, bc_ref)
    conv = []
    for raw, wref, bref, off, wd in zip(raws, wrefs, brefs, offs, widths):
        ext_ref[halo:halo + chunk, off:off + wd] = raw[...]
        acc = jnp.broadcast_to(bref[...], (chunk, wd))
        for j in range(conv_w):
            start = halo - (conv_w - 1) + j
            acc = acc + ext_ref[start:start + chunk, off:off + wd] * wref[j:j + 1, :]
        conv.append(_silu(acc))
    ext_ref[0:halo, :] = ext_ref[chunk:chunk + halo, :]
    xs, bmat, cmat = conv

    dt_r = _softplus(dt_ref[...] + dtb_ref[...])
    a_r = -jnp.exp(al_ref[...])
    da_r = _dot3(dt_r * a_r, triu_ref[...])
    dt_c = dt_r.T
    da_c = da_r.T

    bmat_b = bmat.astype(BF16)
    cmat_b = cmat.astype(BF16)
    row = lax.broadcasted_iota(jnp.int32, (chunk, chunk), 0)
    col = lax.broadcasted_iota(jnp.int32, (chunk, chunk), 1)
    causal = col <= row
    cb = jnp.where(causal, _dot_nt(cmat_b, bmat_b), 0.0)
    bmat_t = bmat.T.astype(BF16)
    first = lax.broadcasted_iota(jnp.int32, (chunk, LANES), 1) < pdim

    heads_per_tile = LANES // pdim
    assert heads_per_tile == 2
    us = []
    ssq = jnp.zeros((chunk, 1), F32)
    for t in range(hpg // heads_per_tile):
        r0 = t * heads_per_tile
        lanes = slice(t * LANES, (t + 1) * LANES)
        da_bc = [jnp.broadcast_to(da_c[:, r:r + 1], (chunk, LANES)) for r in (r0, r0 + 1)]
        dt_bc = [jnp.broadcast_to(dt_c[:, r:r + 1], (chunk, LANES)) for r in (r0, r0 + 1)]
        da_pair = jnp.where(first, da_bc[0], da_bc[1])
        dt_pair = jnp.where(first, dt_bc[0], dt_bc[1])
        last_pair = da_pair[chunk - 1:chunk, :]
        decay = []
        for i, r in enumerate((r0, r0 + 1)):
            diff = da_bc[i] - da_r[r:r + 1, :]
            decay.append((cb * jnp.exp(jnp.minimum(diff, 0.0))).astype(BF16))
        m_cat = jnp.concatenate(decay, axis=1)
        xp = xs[:, lanes]
        xdt = xp * dt_pair
        xbd = jnp.concatenate([jnp.where(first, xdt, 0.0), jnp.where(first, 0.0, xdt)], axis=0).astype(BF16)
        y = _dot(m_cat, xbd)
        ht_old = ht_ref[:, lanes]
        y = y + _dot(cmat_b, ht_old.astype(BF16)) * jnp.exp(da_pair)
        xw = (xdt * jnp.exp(last_pair - da_pair)).astype(BF16)
        ht_ref[:, lanes] = ht_old * jnp.exp(last_pair) + _dot(bmat_t, xw)
        y = y + xp * dskip_ref[:, lanes]
        u = y * _silu(z_ref[:, lanes])
        ssq = ssq + jnp.sum(u * u, axis=1, keepdims=True)
        us.append(u)
    inv = lax.rsqrt(ssq / gw + EPS)
    for t, u in enumerate(us):
        lanes = slice(t * LANES, (t + 1) * LANES)
        o_ref[:, lanes] = (u * inv * gssm_ref[:, lanes]).astype(o_ref.dtype)

    @pl.when(c == pl.num_programs(2) - 1)
    def _():
        st_ref[...] = ht_ref[...]


def _ssd_prompt(zx, dt_raw, conv_w, conv_b, dt_bias, a_log, d_skip, g_ssm, *, batch, seq, dims):
    heads, pdim, nstate, groups = dims
    hpg = heads // groups
    gw = hpg * pdim
    inner = heads * pdim
    chunk = min(LANES, seq)
    n_chunks = seq // chunk
    cw = conv_w.shape[0]
    zx3 = zx.reshape(batch, seq, zx.shape[1])
    dtr = dt_raw.reshape(batch, seq, groups, hpg).transpose(0, 2, 3, 1)
    idx = jnp.arange(chunk)
    triu = (idx[:, None] <= idx[None, :]).astype(BF16)

    zcol0 = 0
    xcol0 = inner // gw
    bcol0 = (2 * inner) // nstate
    ccol0 = (2 * inner + groups * nstate) // nstate
    wb0 = inner // nstate
    wc0 = (inner + groups * nstate) // nstate

    def tok(width, col0):
        return pl.BlockSpec((None, chunk, width), lambda b, g, c: (b, c, col0 + g))

    def par(rows, width, col0):
        return pl.BlockSpec((rows, width), lambda b, g, c: (0, col0 + g))

    def grp(shape):
        return pl.BlockSpec((None,) + shape, lambda b, g, c: (g, 0, 0))

    full = pl.BlockSpec((chunk, chunk), lambda b, g, c: (0, 0))
    conv_b2 = conv_b.reshape(1, -1)
    o_b, st = pl.pallas_call(
        functools.partial(_ssd_prompt_kernel, chunk=chunk, hpg=hpg, pdim=pdim, nstate=nstate, conv_w=cw),
        grid=(batch, groups, n_chunks),
        in_specs=[
            tok(gw, xcol0), tok(nstate, bcol0), tok(nstate, ccol0), tok(gw, zcol0),
            pl.BlockSpec((None, None, hpg, chunk), lambda b, g, c: (b, g, 0, c)),
            par(cw, gw, 0), par(cw, nstate, wb0), par(cw, nstate, wc0),
            par(1, gw, 0), par(1, nstate, wb0), par(1, nstate, wc0),
            grp((hpg, 1)), grp((hpg, 1)),
            par(1, gw, 0), par(1, gw, 0),
            full,
        ],
        out_specs=[
            pl.BlockSpec((None, chunk, gw), lambda b, g, c: (b, c, g)),
            pl.BlockSpec((None, None, nstate, gw), lambda b, g, c: (b, g, 0, 0)),
        ],
        out_shape=[
            jax.ShapeDtypeStruct((batch, seq, inner), BF16),
            jax.ShapeDtypeStruct((batch, groups, nstate, gw), F32),
        ],
        scratch_shapes=[
            pltpu.VMEM((chunk + SUBLANES, gw + 2 * nstate), F32),
            pltpu.VMEM((nstate, gw), F32),
        ],
        compiler_params=_cparams(("parallel", "parallel", "arbitrary")),
        name="ssd_prompt",
    )(zx3, zx3, zx3, zx3, dtr,
      conv_w, conv_w, conv_w, conv_b2, conv_b2, conv_b2,
      dt_bias.reshape(groups, hpg, 1), a_log.reshape(groups, hpg, 1),
      jnp.repeat(d_skip, pdim).reshape(1, inner), g_ssm.reshape(1, inner),
      triu)
    st = st.reshape(batch, groups, nstate, hpg, pdim).transpose(0, 1, 3, 4, 2)
    return o_b.reshape(batch * seq, inner), st.reshape(batch, heads, pdim, nstate)


def _ssd_sample_kernel(xbc_ref, z_ref, dt_ref, cst_ref, h0_ref,
                       cw_ref, cb_ref, dtb_ref, al_ref, dskip_ref, gssm_ref, expand_ref,
                       o_ref, cnew_ref, h_ref, *, inner, nstate, groups, conv_w):
    gw = inner // groups
    x = xbc_ref[...]
    acc = cb_ref[...] + x * cw_ref[conv_w - 1:conv_w, :]
    for j in range(conv_w - 1):
        acc = acc + cst_ref[j:j + 1, :] * cw_ref[j:j + 1, :]
        if j > 0:
            cnew_ref[j - 1:j, :] = cst_ref[j:j + 1, :]
    cnew_ref[conv_w - 2:conv_w - 1, :] = x
    xc = _silu(acc)
    xs = xc[:, :inner]
    bv = xc[:, inner:inner + groups * nstate]
    cv = xc[:, inner + groups * nstate:]

    dt8 = jnp.broadcast_to(dt_ref[...], (SUBLANES, dt_ref.shape[1]))
    dt = _softplus(_dot3(dt8, expand_ref[...])[0:1, :] + dtb_ref[...])
    dec = jnp.exp(dt * (-jnp.exp(al_ref[...])))
    xdt = xs * dt
    row8 = lax.broadcasted_iota(jnp.int32, (SUBLANES, gw), 0)
    ys = []
    for g in range(groups):
        lanes = slice(g * gw, (g + 1) * gw)
        glanes = slice(g * nstate, (g + 1) * nstate)
        stacked = jnp.where(row8 == 0, xdt[:, lanes], jnp.where(row8 == 1, dec[:, lanes], 0.0))
        cols = stacked.T
        hn = h0_ref[lanes, :] * cols[:, 1:2] + cols[:, 0:1] * bv[:, glanes]
        h_ref[lanes, :] = hn
        cg = jnp.broadcast_to(cv[:, glanes], (SUBLANES, nstate)).astype(BF16)
        ys.append(_dot_nt(cg, hn.astype(BF16))[0:1, :])
    y = jnp.concatenate(ys, axis=1) + xs * dskip_ref[...]
    u = y * _silu(z_ref[...])
    for g in range(groups):
        lanes = slice(g * gw, (g + 1) * gw)
        ug = u[:, lanes]
        inv = lax.rsqrt(jnp.mean(ug * ug, axis=1, keepdims=True) + EPS)
        o_ref[:, lanes] = (ug * inv * gssm_ref[:, lanes]).astype(o_ref.dtype)


def _ssd_sample(xbc, z, dt_raw, state_conv, state_ssm, conv_w, conv_b, dt_bias, a_log, d_skip, g_ssm, *, dims):
    heads, pdim, nstate, groups = dims
    inner = heads * pdim
    db, conv_dim = xbc.shape
    cw = conv_w.shape[0]
    rep = lambda v: jnp.repeat(v.astype(F32), pdim).reshape(1, inner)
    expand = (jnp.arange(heads)[:, None] == (jnp.arange(inner)[None, :] // pdim)).astype(BF16)
    row = lambda width: pl.BlockSpec((None, 1, width), lambda b: (b, 0, 0))
    full = lambda shape: pl.BlockSpec(shape, lambda b: (0,) * len(shape))
    o_b, conv_new, h_new = pl.pallas_call(
        functools.partial(_ssd_sample_kernel, inner=inner, nstate=nstate, groups=groups, conv_w=cw),
        grid=(db,),
        in_specs=[
            row(conv_dim), row(inner), row(heads),
            pl.BlockSpec((None, cw - 1, conv_dim), lambda b: (b, 0, 0)),
            pl.BlockSpec((None, inner, nstate), lambda b: (b, 0, 0)),
            full((cw, conv_dim)), full((1, conv_dim)),
            full((1, inner)), full((1, inner)), full((1, inner)), full((1, inner)),
            full((heads, inner)),
        ],
        out_specs=[
            row(inner),
            pl.BlockSpec((None, cw - 1, conv_dim), lambda b: (b, 0, 0)),
            pl.BlockSpec((None, inner, nstate), lambda b: (b, 0, 0)),
        ],
        out_shape=[
            jax.ShapeDtypeStruct((db, 1, inner), BF16),
            jax.ShapeDtypeStruct((db, cw - 1, conv_dim), F32),
            jax.ShapeDtypeStruct((db, inner, nstate), F32),
        ],
        compiler_params=_cparams(("parallel",)),
        name="ssd_sample",
    )(xbc.reshape(db, 1, conv_dim), z.reshape(db, 1, inner), dt_raw.reshape(db, 1, heads),
      state_conv, state_ssm.reshape(db, inner, nstate),
      conv_w, conv_b.reshape(1, conv_dim), rep(dt_bias), rep(a_log), rep(d_skip), g_ssm.reshape(1, inner),
      expand)
    return o_b.reshape(db, inner), conv_new, h_new.reshape(db, heads, pdim, nstate)


def _merge_kernel(oa_ref, ob_ref, wa_ref, wb_ref, ga_ref, gb_ref, o_ref):
    a = _dot(oa_ref[...], wa_ref[...])
    b = _dot(ob_ref[...], wb_ref[...])
    o_ref[...] = (jax.nn.sigmoid(ga_ref[...]) * a + jax.nn.sigmoid(gb_ref[...]) * b).astype(o_ref.dtype)


def _merge(o_a, o_b, wa, wb, gates, *, off_ga, off_gb):
    m, ka = o_a.shape
    kb = o_b.shape[1]
    n = wa.shape[1]
    tm = min(ROW_TILE, m)
    tn = COL_TILE
    ga0, gb0 = off_ga // tn, off_gb // tn
    return pl.pallas_call(
        _merge_kernel,
        grid=(m // tm, n // tn),
        in_specs=[
            pl.BlockSpec((tm, ka), lambda i, j: (i, 0)),
            pl.BlockSpec((tm, kb), lambda i, j: (i, 0)),
            pl.BlockSpec((ka, tn), lambda i, j: (0, j)),
            pl.BlockSpec((kb, tn), lambda i, j: (0, j)),
            pl.BlockSpec((tm, tn), lambda i, j: (i, ga0 + j)),
            pl.BlockSpec((tm, tn), lambda i, j: (i, gb0 + j)),
        ],
        out_specs=pl.BlockSpec((tm, tn), lambda i, j: (i, j)),
        out_shape=jax.ShapeDtypeStruct((m, n), BF16),
        compiler_params=_cparams(("parallel", "arbitrary")),
        name="merge",
    )(o_a, o_b, wa, wb, gates, gates)


def _resid_mm_kernel(x_ref, a_ref, w_ref, g_ref, o_ref, on_ref):
    y = x_ref[...] + _dot(a_ref[...], w_ref[...])
    o_ref[...] = y
    on_ref[...] = _rms_rows(y, g_ref[...]).astype(BF16)


def _resid_matmul_norm(x, a, w, g):
    m, n = x.shape
    k = a.shape[1]
    tm = min(256, m)
    return pl.pallas_call(
        _resid_mm_kernel,
        grid=(m // tm,),
        in_specs=[
            pl.BlockSpec((tm, n), lambda i: (i, 0)),
            pl.BlockSpec((tm, k), lambda i: (i, 0)),
            pl.BlockSpec((k, n), lambda i: (0, 0)),
            pl.BlockSpec((1, n), lambda i: (0, 0)),
        ],
        out_specs=[pl.BlockSpec((tm, n), lambda i: (i, 0)), pl.BlockSpec((tm, n), lambda i: (i, 0))],
        out_shape=[jax.ShapeDtypeStruct((m, n), F32), jax.ShapeDtypeStruct((m, n), BF16)],
        compiler_params=_cparams(("parallel",)),
        name="resid_matmul_norm",
    )(x, a, w, g.reshape(1, n))


def _resid_mm_tiled_kernel(x_ref, a_ref, w_ref, o_ref):
    o_ref[...] = x_ref[...] + _dot(a_ref[...], w_ref[...])


def _resid_matmul(x, a, w):
    m, n = x.shape
    k = a.shape[1]
    tm = min(ROW_TILE, m)
    tn = COL_TILE
    return pl.pallas_call(
        _resid_mm_tiled_kernel,
        grid=(m // tm, n // tn),
        in_specs=[
            pl.BlockSpec((tm, tn), lambda i, j: (i, j)),
            pl.BlockSpec((tm, k), lambda i, j: (i, 0)),
            pl.BlockSpec((k, tn), lambda i, j: (0, j)),
        ],
        out_specs=pl.BlockSpec((tm, tn), lambda i, j: (i, j)),
        out_shape=jax.ShapeDtypeStruct((m, n), F32),
        compiler_params=_cparams(("parallel", "arbitrary")),
        name="resid_matmul",
    )(x, a, w)


def _ffn_up_prompt_kernel(pt_ref, xn_ref, wa_ref, wb_ref, cwa_ref, cwb_ref, cba_ref, cbb_ref, q_ref, *rest,
                          seq, conv_w, pages_per_step, heads, n_pages, total_pages):
    k_refs = rest[:pages_per_step]
    act_ref, ta_ref, tb_ref, z_ref, exta_ref, extb_ref = rest[pages_per_step:]
    i = pl.program_id(1)
    tm = xn_ref.shape[0]
    halo = SUBLANES
    xn = xn_ref[...]
    start_of_seq = (i * tm) % seq == 0
    outs = []
    for w_ref, cw_ref, cb_ref, ext_ref, tail_ref in ((wa_ref, cwa_ref, cba_ref, exta_ref, ta_ref),
                                                     (wb_ref, cwb_ref, cbb_ref, extb_ref, tb_ref)):
        up = _dot(xn, w_ref[...])
        history = ext_ref[0:halo, :]
        ext_ref[0:halo, :] = jnp.where(start_of_seq, jnp.zeros_like(history), history)
        ext_ref[halo:halo + tm, :] = up
        acc = jnp.broadcast_to(cb_ref[...], up.shape)
        for j in range(conv_w):
            start = halo - (conv_w - 1) + j
            acc = acc + ext_ref[start:start + tm, :] * cw_ref[j:j + 1, :]
        tail = ext_ref[tm:tm + halo, :]
        ext_ref[0:halo, :] = tail
        tail_ref[...] = tail
        outs.append(acc)
    act_ref[...] = (_silu(outs[0]) * outs[1]).astype(act_ref.dtype)

    step = pl.program_id(0) * pl.num_programs(1) + i
    own = _own_head(heads, z_ref.shape[1])
    rows = []
    for p in range(pages_per_step):
        flat_page = jnp.minimum(step * pages_per_step + p, total_pages - 1)
        q = q_ref[flat_page // n_pages]
        kp = k_refs[p][...]
        kp = kp.reshape(kp.shape[0] * heads, kp.shape[2]).astype(BF16)
        full = _dot_nt(q, kp)
        rows.append(jnp.sum(jnp.where(own, full, 0.0), axis=0, keepdims=True))
    pad = z_ref.shape[0] - pages_per_step
    if pad:
        rows.append(jnp.zeros((pad, z_ref.shape[1]), F32))
    z_ref[...] = jnp.concatenate(rows, axis=0)


def _ffn_up_prompt(xn, w_up, cw, cb, qn_s, cache_k, page_table, *, seq):
    m, k = xn.shape
    n2 = w_up.shape[1]
    dff = n2 // 2
    tm = min(FFN_ROW_TILE, m, seq)
    tn = COL_TILE
    nj = dff // tn
    ni = m // tm
    cwid = cw.shape[0]
    cb2 = cb.reshape(1, n2)

    db, n_pages = page_table.shape
    _, _, page, heads, hd = cache_k.shape
    total_pages = db * n_pages
    n_steps = nj * ni
    pps = -(-total_pages // n_steps)
    pps_pad = -(-pps // SUBLANES) * SUBLANES

    def page_spec(p):
        return pl.BlockSpec(
            (None, None, page, heads, hd),
            lambda j, i, pt: (0, pt[jnp.minimum((j * ni + i) * pps + p, total_pages - 1)], 0, 0, 0))

    act, ta, tb, z = pl.pallas_call(
        functools.partial(_ffn_up_prompt_kernel, seq=seq, conv_w=cwid, pages_per_step=pps, heads=heads,
                          n_pages=n_pages, total_pages=total_pages),
        grid_spec=pltpu.PrefetchScalarGridSpec(
            num_scalar_prefetch=1,
            grid=(nj, ni),
            in_specs=[
                pl.BlockSpec((tm, k), lambda j, i, pt: (i, 0)),
                pl.BlockSpec((k, tn), lambda j, i, pt: (0, j)),
                pl.BlockSpec((k, tn), lambda j, i, pt: (0, nj + j)),
                pl.BlockSpec((cwid, tn), lambda j, i, pt: (0, j)),
                pl.BlockSpec((cwid, tn), lambda j, i, pt: (0, nj + j)),
                pl.BlockSpec((1, tn), lambda j, i, pt: (0, j)),
                pl.BlockSpec((1, tn), lambda j, i, pt: (0, nj + j)),
                pl.BlockSpec((db, heads, hd), lambda j, i, pt: (0, 0, 0)),
            ] + [page_spec(p) for p in range(pps)],
            out_specs=[
                pl.BlockSpec((tm, tn), lambda j, i, pt: (i, j)),
                pl.BlockSpec((None, SUBLANES, tn), lambda j, i, pt: (i, 0, j)),
                pl.BlockSpec((None, SUBLANES, tn), lambda j, i, pt: (i, 0, j)),
                pl.BlockSpec((None, pps_pad, page * heads), lambda j, i, pt: (j * ni + i, 0, 0)),
            ],
            scratch_shapes=[pltpu.VMEM((tm + SUBLANES, tn), F32), pltpu.VMEM((tm + SUBLANES, tn), F32)],
        ),
        out_shape=[
            jax.ShapeDtypeStruct((m, dff), BF16),
            jax.ShapeDtypeStruct((ni, SUBLANES, dff), F32),
            jax.ShapeDtypeStruct((ni, SUBLANES, dff), F32),
            jax.ShapeDtypeStruct((n_steps, pps_pad, page * heads), F32),
        ],
        compiler_params=_cparams(("parallel", "arbitrary")),
        name="ffn_up_prompt",
    )(page_table.reshape(-1), xn, w_up, w_up, cw, cw, cb2, cb2, qn_s.reshape(db, heads, hd),
      *([cache_k] * pps))
    per_seq = seq // tm
    tails = jnp.concatenate([ta, tb], axis=-1)[per_seq - 1::per_seq, SUBLANES - (cwid - 1):, :]
    z = z[:, :pps].reshape(n_steps * pps, page * heads)[:total_pages]
    return act, tails, z.reshape(db, n_pages, page * heads)


def _ffn_up_sample_kernel(xn_ref, wa_ref, wb_ref, cwa_ref, cwb_ref, cba_ref, cbb_ref, sta_ref, stb_ref,
                          act_ref, ua_ref, ub_ref, *, conv_w):
    xn = xn_ref[...]
    outs = []
    for w_ref, cw_ref, cb_ref, st_ref, u_ref in ((wa_ref, cwa_ref, cba_ref, sta_ref, ua_ref),
                                                 (wb_ref, cwb_ref, cbb_ref, stb_ref, ub_ref)):
        up = _dot(xn, w_ref[...])
        u_ref[...] = up
        acc = cb_ref[...] + up * cw_ref[conv_w - 1:conv_w, :]
        for j in range(conv_w - 1):
            acc = acc + st_ref[j] * cw_ref[j:j + 1, :]
        outs.append(acc)
    act_ref[...] = (_silu(outs[0]) * outs[1]).astype(act_ref.dtype)


def _ffn_up_sample(xn, w_up, cw, cb, state):
    m, k = xn.shape
    n2 = w_up.shape[1]
    dff = n2 // 2
    tn = COL_TILE
    nj = dff // tn
    cwid = cw.shape[0]
    cb2 = cb.reshape(1, n2)
    st = state.transpose(1, 0, 2)
    act, ua, ub = pl.pallas_call(
        functools.partial(_ffn_up_sample_kernel, conv_w=cwid),
        grid=(nj,),
        in_specs=[
            pl.BlockSpec((m, k), lambda j: (0, 0)),
            pl.BlockSpec((k, tn), lambda j: (0, j)),
            pl.BlockSpec((k, tn), lambda j: (0, nj + j)),
            pl.BlockSpec((cwid, tn), lambda j: (0, j)),
            pl.BlockSpec((cwid, tn), lambda j: (0, nj + j)),
            pl.BlockSpec((1, tn), lambda j: (0, j)),
            pl.BlockSpec((1, tn), lambda j: (0, nj + j)),
            pl.BlockSpec((cwid - 1, m, tn), lambda j: (0, 0, j)),
            pl.BlockSpec((cwid - 1, m, tn), lambda j: (0, 0, nj + j)),
        ],
        out_specs=[pl.BlockSpec((m, tn), lambda j: (0, j))] * 3,
        out_shape=[
            jax.ShapeDtypeStruct((m, dff), BF16),
            jax.ShapeDtypeStruct((m, dff), F32),
            jax.ShapeDtypeStruct((m, dff), F32),
        ],
        compiler_params=_cparams(("parallel",)),
        name="ffn_up_sample",
    )(xn, w_up, w_up, cw, cw, cb2, cb2, st, st)
    up = jnp.concatenate([ua, ub], axis=-1)
    new_state = jnp.concatenate([state[:, 1:, :], up[:, None, :]], axis=1)
    return act, new_state


def _ple_kernel(x_ref, xt_ref, p_ref, g_ref, wp_ref, wg_ref, o_ref, xn_ref, *, row_chunk):
    @pl.when(pl.program_id(1) == 0)
    def _():
        tm = x_ref.shape[0]
        for c in range(tm // row_chunk):
            rows = pl.ds(c * row_chunk, row_chunk)
            xn_ref[rows, :] = _rms_rows(x_ref[rows, :], g_ref[...]).astype(BF16)

    emb = _dot(p_ref[...].astype(BF16), wp_ref[...])
    gate = _dot(xn_ref[...], wg_ref[...])
    o_ref[...] = xt_ref[...] + emb * jax.nn.sigmoid(gate)


def _ple(x, p, g, w_ple, w_gate):
    m, d = x.shape
    kp = p.shape[1]
    tm = min(ROW_TILE, m)
    tn = COL_TILE
    row_chunk = min(256, tm)
    return pl.pallas_call(
        functools.partial(_ple_kernel, row_chunk=row_chunk),
        grid=(m // tm, d // tn),
        in_specs=[
            pl.BlockSpec((tm, d), lambda i, j: (i, 0)),
            pl.BlockSpec((tm, tn), lambda i, j: (i, j)),
            pl.BlockSpec((tm, kp), lambda i, j: (i, 0)),
            pl.BlockSpec((1, d), lambda i, j: (0, 0)),
            pl.BlockSpec((kp, tn), lambda i, j: (0, j)),
            pl.BlockSpec((d, tn), lambda i, j: (0, j)),
        ],
        out_specs=pl.BlockSpec((tm, tn), lambda i, j: (i, j)),
        out_shape=jax.ShapeDtypeStruct((m, d), F32),
        scratch_shapes=[pltpu.VMEM((tm, d), BF16)],
        compiler_params=_cparams(("parallel", "arbitrary")),
        name="ple",
    )(x, x, p, g.reshape(1, d), w_ple, w_gate)


def kernel(x_prompt, x_sample, cache_k, cache_v, state_ssm, state_conv, state_ffn_conv, page_table,
           p_prompt, p_sample, g_mix, w_in, g_q, g_k, sb_bias, conv_w, conv_b, dt_bias, a_log, d_skip, g_ssm,
           w_proj_a, w_proj_b, w_out, g_ffn, w_up, ffn_conv_w, ffn_conv_b, w_down, w_ple, g_ple, w_ple_gate):
    depth = w_in.shape[0]
    assert depth == 1
    batch, seq, d_model = x_prompt.shape
    db, dseq, _ = x_sample.shape
    assert dseq == 1
    _, n_pool, page, sb_heads, hd = cache_k.shape
    sb_width = sb_heads * hd
    _, _, heads, pdim, nstate = state_ssm.shape
    inner = heads * pdim
    conv_dim = state_conv.shape[-1]
    groups = (conv_dim - inner) // (2 * nstate)
    dims = (heads, pdim, nstate, groups)

    off_z = 3 * sb_width
    off_dt = off_z + inner + conv_dim
    off_ga = off_dt + heads
    w_in0 = w_in[0]
    w_tail = jnp.concatenate(
        [w_in0[:, off_dt:off_ga], jnp.zeros((d_model, COL_TILE - heads), F32), w_in0[:, off_ga:]], axis=1)
    tail_ga, tail_gb = COL_TILE, COL_TILE + d_model
    wup_b = w_up[0].astype(BF16)
    wa_b = w_proj_a[0].astype(BF16)
    wb_b = w_proj_b[0].astype(BF16)
    wout_b = w_out[0].astype(BF16)
    wdown_b = w_down[0].astype(BF16)
    wple_b = w_ple[0].astype(BF16)
    wgate_b = w_ple_gate[0].astype(BF16)
    gq = g_q[0].reshape(1, hd)
    gk = g_k[0].reshape(1, hd)

    def mixer_inputs(x2):
        xn = _rms_norm_bf16(x2, g_mix[0])
        proj = functools.partial(_proj, xn, w_in0, tn=PROJ_COL_TILE, hd=hd)
        (qn,) = proj(col0=0, ncols=sb_width, epilogue=functools.partial(_epi_q, hd=hd), extras=[gq],
                     out_kinds=["bf16"], name="proj_q")
        kn, knb = proj(col0=sb_width, ncols=sb_width, epilogue=functools.partial(_epi_k, hd=hd), extras=[gk],
                       out_kinds=["heads", "bf16"], name="proj_k")
        v, vbt = proj(col0=2 * sb_width, ncols=sb_width, epilogue=functools.partial(_epi_v, hd=hd), extras=[],
                      out_kinds=["heads", "bf16_t"], name="proj_v")
        (zx,) = proj(col0=off_z, ncols=inner + conv_dim, epilogue=_epi_plain, extras=[],
                     out_kinds=["f32"], name="proj_zx")
        (tail,) = _proj(xn, w_tail, col0=0, ncols=w_tail.shape[1], tn=COL_TILE, epilogue=_epi_plain, extras=[],
                        out_kinds=["f32"], hd=hd, name="proj_tail")
        return qn, kn, knb, v, vbt, zx, tail

    def merge_branches(x2, o_a, o_b, tail):
        merged = _merge(o_a, o_b, wa_b, wb_b, tail, off_ga=tail_ga, off_gb=tail_gb)
        return _resid_matmul_norm(x2, merged, wout_b, g_ffn[0])

    def ffn_down_and_embed(x1, act, p2):
        return _ple(_resid_matmul(x1, act, wdown_b), p2, g_ple[0], wple_b, wgate_b)

    xp2 = x_prompt.reshape(batch * seq, d_model)
    xs2 = x_sample.reshape(db, d_model)
    qn, kn, knb, v, vbt, zx, tail = mixer_inputs(xp2)
    qn_s, kn_s, _, v_s, _, zx_s, tail_s = mixer_inputs(xs2)
    n_pages = page_table.shape[1]

    o_a = _sb_prompt(qn, knb, vbt, sb_bias[0], batch=batch, seq=seq, heads=sb_heads, hd=hd)
    o_b, ssm_p = _ssd_prompt(zx, tail[:, :heads], conv_w[0], conv_b[0], dt_bias[0], a_log[0], d_skip[0], g_ssm[0],
                             batch=batch, seq=seq, dims=dims)
    cw = conv_w.shape[1]
    conv_p = zx.reshape(batch, seq, -1)[:, seq - (cw - 1):, inner:]
    x1, xn1 = merge_branches(xp2, o_a, o_b, tail)
    act, ffn_p, z_s = _ffn_up_prompt(xn1, wup_b, ffn_conv_w[0], ffn_conv_b[0], qn_s, cache_k, page_table, seq=seq)
    yp = ffn_down_and_embed(x1, act, p_prompt[0].reshape(batch * seq, -1))
    k_p = kn.reshape(batch, seq, sb_heads, hd)
    v_p = v.reshape(batch, seq, sb_heads, hd)

    z_s = z_s.reshape(db, n_pages, page, sb_heads).transpose(0, 1, 3, 2)
    w_s = _sb_weights(z_s, sb_bias[0])
    w_s = w_s.transpose(0, 1, 3, 2).reshape(db, n_pages, page * sb_heads)
    o_a_s = _sb_values(w_s, cache_v, page_table, pages_per_step=PAGES_PER_STEP)
    o_b_s, conv_s, ssm_s = _ssd_sample(
        zx_s[:, inner:], zx_s[:, :inner], tail_s[:, :heads], state_conv[0], state_ssm[0],
        conv_w[0], conv_b[0], dt_bias[0], a_log[0], d_skip[0], g_ssm[0], dims=dims)
    x1_s, xn1_s = merge_branches(xs2, o_a_s, o_b_s, tail_s)
    act_s, ffn_s = _ffn_up_sample(xn1_s, wup_b, ffn_conv_w[0], ffn_conv_b[0], state_ffn_conv[0])
    ys = ffn_down_and_embed(x1_s, act_s, p_sample[0].reshape(db, -1))

    return (yp.reshape(batch, seq, d_model), ys.reshape(db, 1, d_model),
            k_p[None], v_p[None], ssm_p[None], conv_p[None], ffn_p[None],
            kn_s.reshape(db, 1, sb_heads, hd)[None], v_s.reshape(db, 1, sb_heads, hd)[None],
            ssm_s[None], conv_s[None], ffn_s[None])
```

```python
import functools
import math

import jax
import jax.numpy as jnp
from jax import lax
from jax.experimental import pallas as pl
from jax.experimental.pallas import tpu as pltpu

F32 = jnp.float32
BF16 = jnp.bfloat16
EPS = 1e-6
LANES = 128
SUBLANES = 8
VMEM_LIMIT = 56 * 1024 * 1024
COL_TILE = 512
ROW_TILE = 1024
PROJ_COL_TILE = 1024
ATTN_BLOCK = 256
ATTN_HEADS_PER_STEP = 4
FFN_ROW_TILE = 512


def _cparams(sem):
    return pltpu.CompilerParams(dimension_semantics=sem, vmem_limit_bytes=VMEM_LIMIT)


def _dot(a, b):
    return jnp.dot(a, b, preferred_element_type=F32)


def _dot_nt(a, b):
    return lax.dot_general(a, b, (((1,), (1,)), ((), ())), preferred_element_type=F32)


def _split3(x):
    hi = x.astype(BF16)
    r = x - hi.astype(F32)
    mid = r.astype(BF16)
    lo = (r - mid.astype(F32)).astype(BF16)
    return hi, mid, lo


def _dot3(x, m):
    hi, mid, lo = _split3(x)
    return _dot(hi, m) + _dot(mid, m) + _dot(lo, m)


def _dot3_r(m, x):
    hi, mid, lo = _split3(x)
    return _dot(m, hi) + _dot(m, mid) + _dot(m, lo)


def _softplus(x):
    return jnp.maximum(x, 0.0) + jnp.log(1.0 + jnp.exp(-jnp.abs(x)))


LOG2E = math.log2(math.e)


def _softplus2(x):
    neg_abs = lax.bitcast_convert_type(lax.bitcast_convert_type(x, jnp.uint32) | jnp.uint32(0x80000000), F32)
    return jnp.maximum(x, 0.0) + jnp.log2(1.0 + jnp.exp2(neg_abs))


def _silu(x):
    return x * jax.nn.sigmoid(x)


def _rms_rows(x, g):
    ms = jnp.mean(x * x, axis=-1, keepdims=True)
    return x * lax.rsqrt(ms + EPS) * g


def _rms_norm_kernel(x_ref, g_ref, o_ref):
    o_ref[...] = _rms_rows(x_ref[...], g_ref[...]).astype(BF16)


def _rms_norm_bf16(x, g):
    m, k = x.shape
    tm = min(256, m)
    return pl.pallas_call(
        _rms_norm_kernel,
        grid=(m // tm,),
        in_specs=[pl.BlockSpec((tm, k), lambda i: (i, 0)), pl.BlockSpec((1, k), lambda i: (0, 0))],
        out_specs=pl.BlockSpec((tm, k), lambda i: (i, 0)),
        out_shape=jax.ShapeDtypeStruct((m, k), BF16),
        compiler_params=_cparams(("parallel",)),
        name="rms_norm",
    )(x, g.reshape(1, k))


def _proj_kernel(x_ref, w_ref, *rest, n_extra, n_out, epilogue):
    extras = rest[:n_extra]
    outs = rest[n_extra:n_extra + n_out]
    wb_ref = rest[n_extra + n_out]

    @pl.when(pl.program_id(1) == 0)
    def _():
        wb_ref[...] = w_ref[...].astype(BF16)

    epilogue(_dot(x_ref[...], wb_ref[...]), *extras, *outs)


def _epi_plain(acc, o_ref):
    o_ref[...] = acc


def _epi_q(acc, g_ref, qn_ref, *, hd):
    scale = LOG2E / math.sqrt(hd)
    for h in range(acc.shape[1] // hd):
        lanes = slice(h * hd, (h + 1) * hd)
        qn_ref[:, lanes] = (_rms_rows(acc[:, lanes], g_ref[...]) * scale).astype(BF16)


def _epi_k(acc, g_ref, kn_ref, knb_ref, *, hd):
    for h in range(acc.shape[1] // hd):
        lanes = slice(h * hd, (h + 1) * hd)
        kn = _rms_rows(acc[:, lanes], g_ref[...])
        kn_ref[:, h, :] = kn
        knb_ref[:, lanes] = kn.astype(BF16)


def _epi_v(acc, v_ref, vbt_ref, *, hd):
    for h in range(acc.shape[1] // hd):
        lanes = slice(h * hd, (h + 1) * hd)
        v_ref[:, h, :] = acc[:, lanes]
        vbt_ref[lanes, :] = acc[:, lanes].T.astype(BF16)


def _proj(xn, w, *, col0, ncols, tn, epilogue, extras, out_kinds, hd, name):
    m, k = xn.shape
    tm = min(ROW_TILE, m)
    tn = min(tn, ncols)
    assert col0 % tn == 0 and ncols % tn == 0
    c0 = col0 // tn
    out_specs, out_shapes = [], []
    for kind in out_kinds:
        if kind == "heads":
            out_specs.append(pl.BlockSpec((tm, tn // hd, hd), lambda j, i: (i, j, 0)))
            out_shapes.append(jax.ShapeDtypeStruct((m, ncols // hd, hd), F32))
        elif kind == "bf16_t":
            out_specs.append(pl.BlockSpec((tn, tm), lambda j, i: (j, i)))
            out_shapes.append(jax.ShapeDtypeStruct((ncols, m), BF16))
        else:
            out_specs.append(pl.BlockSpec((tm, tn), lambda j, i: (i, j)))
            out_shapes.append(jax.ShapeDtypeStruct((m, ncols), F32 if kind == "f32" else BF16))
    extra_specs = [pl.BlockSpec(e.shape, lambda j, i: (0, 0)) for e in extras]
    return pl.pallas_call(
        functools.partial(_proj_kernel, n_extra=len(extras), n_out=len(out_kinds), epilogue=epilogue),
        grid=(ncols // tn, m // tm),
        in_specs=[pl.BlockSpec((tm, k), lambda j, i: (i, 0)),
                  pl.BlockSpec((k, tn), lambda j, i: (0, c0 + j))] + extra_specs,
        out_specs=out_specs,
        out_shape=out_shapes,
        scratch_shapes=[pltpu.VMEM((k, tn), BF16)],
        compiler_params=_cparams(("parallel", "arbitrary")),
        name=name,
    )(xn, w, *extras)


def _sb_prompt_kernel(bias_ref, q_ref, k_ref, vt_ref, tri_ref, o_ref, *, tq, tk, hd, hps):
    hb = pl.program_id(1)
    qi = pl.program_id(2)
    tri = tri_ref[...]

    def block(k0, state, masked):
        if masked:
            row = lax.broadcasted_iota(jnp.int32, (tq, tk), 0)
            col = lax.broadcasted_iota(jnp.int32, (tq, tk), 1)
            keep = col < row
        heads = range(hps)
        lanes = [slice(g * hd, (g + 1) * hd) for g in heads]
        zs = [_dot_nt(q_ref[:, lanes[g]], k_ref[pl.ds(k0, tk), lanes[g]]) + bias_ref[hb * hps + g] for g in heads]
        sps = [_softplus2(z) for z in zs]
        if masked:
            sps = [jnp.where(keep, sp, 0.0) for sp in sps]
        spb = [sp.astype(BF16) for sp in sps]
        tails = [_dot(sb, tri) for sb in spb]
        ws = [jnp.exp2(zs[g] - sps[g] - tails[g] - state[g][0]) for g in heads]
        if masked:
            ws = [jnp.where(keep, w, 0.0) for w in ws]
        pvs = [_dot_nt(vt_ref[lanes[g], pl.ds(k0, tk)], ws[g].astype(BF16)) for g in heads]
        sums = [tails[g][:, 0:1] + spb[g][:, 0:1].astype(F32) for g in heads]
        return tuple((state[g][0] + sums[g], state[g][1] + pvs[g]) for g in heads)

    state = tuple((jnp.zeros((tq, 1), F32), jnp.zeros((hd, tq), F32)) for _ in range(hps))
    state = block(pl.multiple_of(qi * tq, tq), state, True)

    def body(it, st):
        return block(pl.multiple_of((qi - 1 - it) * tk, tk), st, False)

    state = lax.fori_loop(0, qi, body, state)
    for g in range(hps):
        o_ref[:, g * hd:(g + 1) * hd] = state[g][1].T.astype(o_ref.dtype)


def _sb_prompt(qn, knb, vbt, sb_bias, *, batch, seq, heads, hd):
    tq = tk = min(ATTN_BLOCK, seq)
    hps = ATTN_HEADS_PER_STEP
    assert seq % tq == 0 and heads % hps == 0
    width = heads * hd
    q3 = qn.reshape(batch, seq, width)
    k3 = knb.reshape(batch, seq, width)
    tri = (jnp.arange(tk)[:, None] > jnp.arange(tk)[None, :]).astype(BF16)
    out = pl.pallas_call(
        functools.partial(_sb_prompt_kernel, tq=tq, tk=tk, hd=hd, hps=hps),
        grid=(batch, heads // hps, seq // tq),
        in_specs=[
            pl.BlockSpec(memory_space=pltpu.SMEM),
            pl.BlockSpec((None, tq, hps * hd), lambda b, h, i: (b, i, h)),
            pl.BlockSpec((None, seq, hps * hd), lambda b, h, i: (b, 0, h)),
            pl.BlockSpec((hps * hd, seq), lambda b, h, i: (h, b)),
            pl.BlockSpec((tk, tk), lambda b, h, i: (0, 0)),
        ],
        out_specs=pl.BlockSpec((None, tq, hps * hd), lambda b, h, i: (b, i, h)),
        out_shape=jax.ShapeDtypeStruct((batch, seq, width), BF16),
        compiler_params=_cparams(("parallel", "parallel", "arbitrary")),
        name="sb_attn_prompt",
    )(sb_bias.astype(F32) * LOG2E, q3, k3, vbt, tri)
    return out.reshape(batch * seq, width)


class _PageJob:
    def __init__(self, cache, page_table, n_steps):
        self.db, self.n_pages = page_table.shape
        _, _, self.page, self.heads, self.hd = cache.shape
        self.total = self.db * self.n_pages
        self.n_steps = n_steps
        self.pps = -(-self.total // n_steps)
        self.pps_pad = -(-self.pps // SUBLANES) * SUBLANES
        self.width = self.page * self.heads

    def page_specs(self, step_of):
        def spec(p):
            def index(*args):
                pt = args[-1]
                flat = jnp.minimum(step_of(*args[:-1]) * self.pps + p, self.total - 1)
                return (0, pt[flat], 0, 0, 0)
            return pl.BlockSpec((None, None, self.page, self.heads, self.hd), index)
        return [spec(p) for p in range(self.pps)]

    def sequence_of(self, step, p):
        return jnp.minimum(step * self.pps + p, self.total - 1) // self.n_pages

    def rows_to_steps(self, rows):
        rows = jnp.pad(rows, ((0, self.n_steps * self.pps - self.total), (0, 0)))
        rows = rows.reshape(self.n_steps, self.pps, self.width)
        return jnp.pad(rows, ((0, 0), (0, self.pps_pad - self.pps), (0, 0)))

    def steps_to_rows(self, blocks):
        return blocks[:, :self.pps].reshape(self.n_steps * self.pps, self.width)[:self.total]


def _own_head(heads, width):
    row = lax.broadcasted_iota(jnp.int32, (heads, width), 0)
    col = lax.broadcasted_iota(jnp.int32, (heads, width), 1)
    return col % heads == row


def _flat_page(page_ref, heads):
    kp = page_ref[...]
    return kp.reshape(kp.shape[0] * heads, kp.shape[2]).astype(BF16)


def _score_page(job, step, p, q_ref, k_refs):
    q = q_ref[job.sequence_of(step, p)]
    full = _dot_nt(q, _flat_page(k_refs[p], job.heads))
    return jnp.sum(jnp.where(_own_head(job.heads, job.width), full, 0.0), axis=0, keepdims=True)


def _store_scores(job, rows, z_ref):
    if job.pps_pad > job.pps:
        rows = rows + [jnp.zeros((job.pps_pad - job.pps, job.width), F32)]
    z_ref[...] = jnp.concatenate(rows, axis=0)


def _sum_weighted_pages(job, step, w_ref, v_refs, acc_ref):
    own = _own_head(job.heads, job.width)
    w_all = w_ref[...]
    for p in range(job.pps):
        wm = jnp.where(own, jnp.broadcast_to(w_all[p:p + 1, :], (job.heads, job.width)), 0.0).astype(BF16)
        b = job.sequence_of(step, p)
        acc_ref[b] = acc_ref[b] + _dot(wm, _flat_page(v_refs[p], job.heads))


def _sb_weights_kernel(z_ref, bias_ref, tri_ref, ones_ref, mpage_ref, w_ref):
    n_pages, heads, page = z_ref.shape
    z = z_ref[...] + bias_ref[...]
    sp = _softplus2(z)
    log_1mb = (-sp).reshape(n_pages * heads, page)
    log_b = (z - sp).reshape(n_pages * heads, page)
    within = _dot3(log_1mb, tri_ref[...])
    totals = _dot3(log_1mb, ones_ref[...])
    later = _dot3_r(mpage_ref[...], totals)
    w = jnp.exp2(log_b + within + later)
    w_ref[...] = w.reshape(n_pages, heads, page)


def _sb_weights(z, sb_bias):
    db, n_pages, heads, page = z.shape
    rows = n_pages * heads
    tri = (jnp.arange(page)[:, None] > jnp.arange(page)[None, :]).astype(BF16)
    ones = jnp.ones((page, page), BF16)
    r = jnp.arange(rows)
    mpage = ((r[None, :] // heads > r[:, None] // heads) & (r[None, :] % heads == r[:, None] % heads)).astype(BF16)
    full = lambda shape: pl.BlockSpec(shape, lambda b: (0,) * len(shape))
    return pl.pallas_call(
        _sb_weights_kernel,
        grid=(db,),
        in_specs=[
            pl.BlockSpec((None, n_pages, heads, page), lambda b: (b, 0, 0, 0)),
            full((1, heads, 1)),
            full((page, page)),
            full((page, page)),
            full((rows, rows)),
        ],
        out_specs=pl.BlockSpec((None, n_pages, heads, page), lambda b: (b, 0, 0, 0)),
        out_shape=jax.ShapeDtypeStruct((db, n_pages, heads, page), F32),
        compiler_params=_cparams(("parallel",)),
        name="sb_sample_weights",
    )(z, (sb_bias.astype(F32) * LOG2E).reshape(1, heads, 1), tri, ones, mpage)


def _ssd_prompt_kernel(pt_ref, x_ref, bm_ref, cm_ref, z_ref, dt_ref,
                       wx_ref, wb_ref, wc_ref, bx_ref, bb_ref, bc_ref,
                       dtb_ref, al_ref, dskip_ref, gssm_ref, triu_ref, q_ref, *rest,
                       chunk, hpg, pdim, nstate, conv_w, job):
    k_refs = rest[:job.pps]
    o_ref, st_ref, sc_ref, ext_ref, ht_ref = rest[job.pps:]
    c = pl.program_id(2)
    gw = hpg * pdim
    halo = SUBLANES
    widths = (gw, nstate, nstate)
    offs = (0, gw, gw + nstate)

    @pl.when(c == 0)
    def _():
        ext_ref[0:halo, :] = jnp.zeros((halo, ext_ref.shape[1]), F32)
        ht_ref[...] = jnp.zeros_like(ht_ref)

    raws = (x_ref, bm_ref, cm_ref)
    wrefs = (wx_ref, wb_ref, wc_ref)
    brefs = (bx_ref, bb_ref, bc_ref)
    conv = []
    for raw, wref, bref, off, wd in zip(raws, wrefs, brefs, offs, widths):
        ext_ref[halo:halo + chunk, off:off + wd] = raw[...]
        acc = jnp.broadcast_to(bref[...], (chunk, wd))
        for j in range(conv_w):
            start = halo - (conv_w - 1) + j
            acc = acc + ext_ref[start:start + chunk, off:off + wd] * wref[j:j + 1, :]
        conv.append(_silu(acc))
    ext_ref[0:halo, :] = ext_ref[chunk:chunk + halo, :]
    xs, bmat, cmat = conv

    dt_r = _softplus(dt_ref[...] + dtb_ref[...])
    a_r = -jnp.exp(al_ref[...])
    da_r = _dot3(dt_r * a_r, triu_ref[...])
    dt_c = dt_r.T
    da_c = da_r.T

    bmat_b = bmat.astype(BF16)
    cmat_b = cmat.astype(BF16)
    row = lax.broadcasted_iota(jnp.int32, (chunk, chunk), 0)
    col = lax.broadcasted_iota(jnp.int32, (chunk, chunk), 1)
    causal = col <= row
    cb = jnp.where(causal, _dot_nt(cmat_b, bmat_b), 0.0)
    bmat_t = bmat.T.astype(BF16)
    first = lax.broadcasted_iota(jnp.int32, (chunk, LANES), 1) < pdim

    heads_per_tile = LANES // pdim
    assert heads_per_tile == 2
    n_tiles = hpg // heads_per_tile
    step = (pl.program_id(0) * pl.num_programs(1) + pl.program_id(1)) * pl.num_programs(2) + c
    scores = []
    us = []
    ssq = jnp.zeros((chunk, 1), F32)
    for t in range(n_tiles):
        for p in range(t * job.pps // n_tiles, (t + 1) * job.pps // n_tiles):
            scores.append(_score_page(job, step, p, q_ref, k_refs))
        r0 = t * heads_per_tile
        lanes = slice(t * LANES, (t + 1) * LANES)
        da_bc = [jnp.broadcast_to(da_c[:, r:r + 1], (chunk, LANES)) for r in (r0, r0 + 1)]
        dt_bc = [jnp.broadcast_to(dt_c[:, r:r + 1], (chunk, LANES)) for r in (r0, r0 + 1)]
        da_pair = jnp.where(first, da_bc[0], da_bc[1])
        dt_pair = jnp.where(first, dt_bc[0], dt_bc[1])
        last_pair = da_pair[chunk - 1:chunk, :]
        decay = []
        for i, r in enumerate((r0, r0 + 1)):
            diff = da_bc[i] - da_r[r:r + 1, :]
            decay.append((cb * jnp.exp(jnp.minimum(diff, 0.0))).astype(BF16))
        m_cat = jnp.concatenate(decay, axis=1)
        xp = xs[:, lanes]
        xdt = xp * dt_pair
        xbd = jnp.concatenate([jnp.where(first, xdt, 0.0), jnp.where(first, 0.0, xdt)], axis=0).astype(BF16)
        y = _dot(m_cat, xbd)
        ht_old = ht_ref[:, lanes]
        y = y + _dot(cmat_b, ht_old.astype(BF16)) * jnp.exp(da_pair)
        xw = (xdt * jnp.exp(last_pair - da_pair)).astype(BF16)
        ht_ref[:, lanes] = ht_old * jnp.exp(last_pair) + _dot(bmat_t, xw)
        y = y + xp * dskip_ref[:, lanes]
        u = y * _silu(z_ref[:, lanes])
        ssq = ssq + jnp.sum(u * u, axis=1, keepdims=True)
        us.append(u)
    inv = lax.rsqrt(ssq / gw + EPS)
    for t, u in enumerate(us):
        lanes = slice(t * LANES, (t + 1) * LANES)
        o_ref[:, lanes] = (u * inv * gssm_ref[:, lanes]).astype(o_ref.dtype)
    _store_scores(job, scores, sc_ref)

    @pl.when(c == pl.num_programs(2) - 1)
    def _():
        st_ref[...] = ht_ref[...]


def _ssd_prompt(zx, dt_raw, conv_w, conv_b, dt_bias, a_log, d_skip, g_ssm, qn_s, cache_k, page_table, *,
                batch, seq, dims):
    heads, pdim, nstate, groups = dims
    hpg = heads // groups
    gw = hpg * pdim
    inner = heads * pdim
    chunk = min(LANES, seq)
    n_chunks = seq // chunk
    cw = conv_w.shape[0]
    zx3 = zx.reshape(batch, seq, zx.shape[1])
    dtt = dt_raw.T
    idx = jnp.arange(chunk)
    triu = (idx[:, None] <= idx[None, :]).astype(BF16)
    job = _PageJob(cache_k, page_table, batch * groups * n_chunks)

    zcol0 = 0
    xcol0 = inner // gw
    bcol0 = (2 * inner) // nstate
    ccol0 = (2 * inner + groups * nstate) // nstate
    wb0 = inner // nstate
    wc0 = (inner + groups * nstate) // nstate

    def step_of(b, g, c):
        return (b * groups + g) * n_chunks + c

    def tok(width, col0):
        return pl.BlockSpec((None, chunk, width), lambda b, g, c, pt: (b, c, col0 + g))

    def par(rows, width, col0):
        return pl.BlockSpec((rows, width), lambda b, g, c, pt: (0, col0 + g))

    def grp(shape):
        return pl.BlockSpec((None,) + shape, lambda b, g, c, pt: (g, 0, 0))

    conv_b2 = conv_b.reshape(1, -1)
    o_b, st, scores = pl.pallas_call(
        functools.partial(_ssd_prompt_kernel, chunk=chunk, hpg=hpg, pdim=pdim, nstate=nstate, conv_w=cw, job=job),
        grid_spec=pltpu.PrefetchScalarGridSpec(
            num_scalar_prefetch=1,
            grid=(batch, groups, n_chunks),
            in_specs=[
                tok(gw, xcol0), tok(nstate, bcol0), tok(nstate, ccol0), tok(gw, zcol0),
                pl.BlockSpec((hpg, chunk), lambda b, g, c, pt: (g, b * n_chunks + c)),
                par(cw, gw, 0), par(cw, nstate, wb0), par(cw, nstate, wc0),
                par(1, gw, 0), par(1, nstate, wb0), par(1, nstate, wc0),
                grp((hpg, 1)), grp((hpg, 1)),
                par(1, gw, 0), par(1, gw, 0),
                pl.BlockSpec((chunk, chunk), lambda b, g, c, pt: (0, 0)),
                pl.BlockSpec((job.db, job.heads, job.hd), lambda b, g, c, pt: (0, 0, 0)),
            ] + job.page_specs(step_of),
            out_specs=[
                pl.BlockSpec((None, chunk, gw), lambda b, g, c, pt: (b, c, g)),
                pl.BlockSpec((None, None, nstate, gw), lambda b, g, c, pt: (b, g, 0, 0)),
                pl.BlockSpec((None, job.pps_pad, job.width), lambda b, g, c, pt: (step_of(b, g, c), 0, 0)),
            ],
            scratch_shapes=[
                pltpu.VMEM((chunk + SUBLANES, gw + 2 * nstate), F32),
                pltpu.VMEM((nstate, gw), F32),
            ],
        ),
        out_shape=[
            jax.ShapeDtypeStruct((batch, seq, inner), BF16),
            jax.ShapeDtypeStruct((batch, groups, nstate, gw), F32),
            jax.ShapeDtypeStruct((job.n_steps, job.pps_pad, job.width), F32),
        ],
        compiler_params=_cparams(("parallel", "parallel", "arbitrary")),
        name="ssd_prompt",
    )(page_table.reshape(-1), zx3, zx3, zx3, zx3, dtt,
      conv_w, conv_w, conv_w, conv_b2, conv_b2, conv_b2,
      dt_bias.reshape(groups, hpg, 1), a_log.reshape(groups, hpg, 1),
      jnp.repeat(d_skip, pdim).reshape(1, inner), g_ssm.reshape(1, inner),
      triu, qn_s.reshape(job.db, job.heads, job.hd), *([cache_k] * job.pps))
    st = st.reshape(batch, groups, nstate, hpg, pdim).transpose(0, 1, 3, 4, 2)
    scores = job.steps_to_rows(scores).reshape(job.db, job.n_pages, job.width)
    return o_b.reshape(batch * seq, inner), st.reshape(batch, heads, pdim, nstate), scores


def _ssd_sample_kernel(xbc_ref, z_ref, dt_ref, cst_ref, h0_ref,
                       cw_ref, cb_ref, dtb_ref, al_ref, dskip_ref, gssm_ref, expand_ref,
                       o_ref, cnew_ref, h_ref, *, inner, nstate, groups, conv_w):
    gw = inner // groups
    x = xbc_ref[...]
    acc = cb_ref[...] + x * cw_ref[conv_w - 1:conv_w, :]
    for j in range(conv_w - 1):
        acc = acc + cst_ref[j:j + 1, :] * cw_ref[j:j + 1, :]
        if j > 0:
            cnew_ref[j - 1:j, :] = cst_ref[j:j + 1, :]
    cnew_ref[conv_w - 2:conv_w - 1, :] = x
    xc = _silu(acc)
    xs = xc[:, :inner]
    bv = xc[:, inner:inner + groups * nstate]
    cv = xc[:, inner + groups * nstate:]

    dt8 = jnp.broadcast_to(dt_ref[...], (SUBLANES, dt_ref.shape[1]))
    dt = _softplus(_dot3(dt8, expand_ref[...])[0:1, :] + dtb_ref[...])
    dec = jnp.exp(dt * (-jnp.exp(al_ref[...])))
    xdt = xs * dt
    row8 = lax.broadcasted_iota(jnp.int32, (SUBLANES, gw), 0)
    ys = []
    for g in range(groups):
        lanes = slice(g * gw, (g + 1) * gw)
        glanes = slice(g * nstate, (g + 1) * nstate)
        stacked = jnp.where(row8 == 0, xdt[:, lanes], jnp.where(row8 == 1, dec[:, lanes], 0.0))
        cols = stacked.T
        hn = h0_ref[lanes, :] * cols[:, 1:2] + cols[:, 0:1] * bv[:, glanes]
        h_ref[lanes, :] = hn
        cg = jnp.broadcast_to(cv[:, glanes], (SUBLANES, nstate)).astype(BF16)
        ys.append(_dot_nt(cg, hn.astype(BF16))[0:1, :])
    y = jnp.concatenate(ys, axis=1) + xs * dskip_ref[...]
    u = y * _silu(z_ref[...])
    for g in range(groups):
        lanes = slice(g * gw, (g + 1) * gw)
        ug = u[:, lanes]
        inv = lax.rsqrt(jnp.mean(ug * ug, axis=1, keepdims=True) + EPS)
        o_ref[:, lanes] = (ug * inv * gssm_ref[:, lanes]).astype(o_ref.dtype)


def _ssd_sample(xbc, z, dt_raw, state_conv, state_ssm, conv_w, conv_b, dt_bias, a_log, d_skip, g_ssm, *, dims):
    heads, pdim, nstate, groups = dims
    inner = heads * pdim
    db, conv_dim = xbc.shape
    cw = conv_w.shape[0]
    rep = lambda v: jnp.repeat(v.astype(F32), pdim).reshape(1, inner)
    expand = (jnp.arange(heads)[:, None] == (jnp.arange(inner)[None, :] // pdim)).astype(BF16)
    row = lambda width: pl.BlockSpec((None, 1, width), lambda b: (b, 0, 0))
    full = lambda shape: pl.BlockSpec(shape, lambda b: (0,) * len(shape))
    o_b, conv_new, h_new = pl.pallas_call(
        functools.partial(_ssd_sample_kernel, inner=inner, nstate=nstate, groups=groups, conv_w=cw),
        grid=(db,),
        in_specs=[
            row(conv_dim), row(inner), row(heads),
            pl.BlockSpec((None, cw - 1, conv_dim), lambda b: (b, 0, 0)),
            pl.BlockSpec((None, inner, nstate), lambda b: (b, 0, 0)),
            full((cw, conv_dim)), full((1, conv_dim)),
            full((1, inner)), full((1, inner)), full((1, inner)), full((1, inner)),
            full((heads, inner)),
        ],
        out_specs=[
            row(inner),
            pl.BlockSpec((None, cw - 1, conv_dim), lambda b: (b, 0, 0)),
            pl.BlockSpec((None, inner, nstate), lambda b: (b, 0, 0)),
        ],
        out_shape=[
            jax.ShapeDtypeStruct((db, 1, inner), BF16),
            jax.ShapeDtypeStruct((db, cw - 1, conv_dim), F32),
            jax.ShapeDtypeStruct((db, inner, nstate), F32),
        ],
        compiler_params=_cparams(("parallel",)),
        name="ssd_sample",
    )(xbc.reshape(db, 1, conv_dim), z.reshape(db, 1, inner), dt_raw.reshape(db, 1, heads),
      state_conv, state_ssm.reshape(db, inner, nstate),
      conv_w, conv_b.reshape(1, conv_dim), rep(dt_bias), rep(a_log), rep(d_skip), g_ssm.reshape(1, inner),
      expand)
    return o_b.reshape(db, inner), conv_new, h_new.reshape(db, heads, pdim, nstate)


def _merge_kernel(oa_ref, ob_ref, wa_ref, wb_ref, ga_ref, gb_ref, o_ref):
    a = _dot(oa_ref[...], wa_ref[...])
    b = _dot(ob_ref[...], wb_ref[...])
    o_ref[...] = (jax.nn.sigmoid(ga_ref[...]) * a + jax.nn.sigmoid(gb_ref[...]) * b).astype(o_ref.dtype)


def _merge(o_a, o_b, wa, wb, gates, *, off_ga, off_gb):
    m, ka = o_a.shape
    kb = o_b.shape[1]
    n = wa.shape[1]
    tm = min(ROW_TILE, m)
    tn = COL_TILE
    ga0, gb0 = off_ga // tn, off_gb // tn
    return pl.pallas_call(
        _merge_kernel,
        grid=(m // tm, n // tn),
        in_specs=[
            pl.BlockSpec((tm, ka), lambda i, j: (i, 0)),
            pl.BlockSpec((tm, kb), lambda i, j: (i, 0)),
            pl.BlockSpec((ka, tn), lambda i, j: (0, j)),
            pl.BlockSpec((kb, tn), lambda i, j: (0, j)),
            pl.BlockSpec((tm, tn), lambda i, j: (i, ga0 + j)),
            pl.BlockSpec((tm, tn), lambda i, j: (i, gb0 + j)),
        ],
        out_specs=pl.BlockSpec((tm, tn), lambda i, j: (i, j)),
        out_shape=jax.ShapeDtypeStruct((m, n), BF16),
        compiler_params=_cparams(("parallel", "arbitrary")),
        name="merge",
    )(o_a, o_b, wa, wb, gates, gates)


def _resid_mm_kernel(x_ref, a_ref, w_ref, g_ref, o_ref, on_ref):
    y = x_ref[...] + _dot(a_ref[...], w_ref[...])
    o_ref[...] = y
    on_ref[...] = _rms_rows(y, g_ref[...]).astype(BF16)


def _resid_matmul_norm(x, a, w, g):
    m, n = x.shape
    k = a.shape[1]
    tm = min(256, m)
    return pl.pallas_call(
        _resid_mm_kernel,
        grid=(m // tm,),
        in_specs=[
            pl.BlockSpec((tm, n), lambda i: (i, 0)),
            pl.BlockSpec((tm, k), lambda i: (i, 0)),
            pl.BlockSpec((k, n), lambda i: (0, 0)),
            pl.BlockSpec((1, n), lambda i: (0, 0)),
        ],
        out_specs=[pl.BlockSpec((tm, n), lambda i: (i, 0)), pl.BlockSpec((tm, n), lambda i: (i, 0))],
        out_shape=[jax.ShapeDtypeStruct((m, n), F32), jax.ShapeDtypeStruct((m, n), BF16)],
        compiler_params=_cparams(("parallel",)),
        name="resid_matmul_norm",
    )(x, a, w, g.reshape(1, n))


def _resid_mm_tiled_kernel(x_ref, a_ref, w_ref, o_ref):
    o_ref[...] = x_ref[...] + _dot(a_ref[...], w_ref[...])


def _resid_matmul(x, a, w):
    m, n = x.shape
    k = a.shape[1]
    tm = min(ROW_TILE, m)
    tn = COL_TILE
    return pl.pallas_call(
        _resid_mm_tiled_kernel,
        grid=(m // tm, n // tn),
        in_specs=[
            pl.BlockSpec((tm, tn), lambda i, j: (i, j)),
            pl.BlockSpec((tm, k), lambda i, j: (i, 0)),
            pl.BlockSpec((k, tn), lambda i, j: (0, j)),
        ],
        out_specs=pl.BlockSpec((tm, tn), lambda i, j: (i, j)),
        out_shape=jax.ShapeDtypeStruct((m, n), F32),
        compiler_params=_cparams(("parallel", "arbitrary")),
        name="resid_matmul",
    )(x, a, w)


def _ffn_up_prompt_kernel(pt_ref, xn_ref, wa_ref, wb_ref, cwa_ref, cwb_ref, cba_ref, cbb_ref, pw_ref, *rest,
                          seq, conv_w, job):
    v_refs = rest[:job.pps]
    act_ref, ta_ref, tb_ref, oa_ref, exta_ref, extb_ref, acc_ref = rest[job.pps:]
    i = pl.program_id(1)
    step = pl.program_id(0) * pl.num_programs(1) + i
    tm = xn_ref.shape[0]
    halo = SUBLANES

    @pl.when(step == 0)
    def _():
        acc_ref[...] = jnp.zeros_like(acc_ref)

    xn = xn_ref[...]
    start_of_seq = (i * tm) % seq == 0
    outs = []
    for w_ref, cw_ref, cb_ref, ext_ref, tail_ref in ((wa_ref, cwa_ref, cba_ref, exta_ref, ta_ref),
                                                     (wb_ref, cwb_ref, cbb_ref, extb_ref, tb_ref)):
        up = _dot(xn, w_ref[...])
        history = ext_ref[0:halo, :]
        ext_ref[0:halo, :] = jnp.where(start_of_seq, jnp.zeros_like(history), history)
        ext_ref[halo:halo + tm, :] = up
        acc = jnp.broadcast_to(cb_ref[...], up.shape)
        for j in range(conv_w):
            start = halo - (conv_w - 1) + j
            acc = acc + ext_ref[start:start + tm, :] * cw_ref[j:j + 1, :]
        tail = ext_ref[tm:tm + halo, :]
        ext_ref[0:halo, :] = tail
        tail_ref[...] = tail
        outs.append(acc)
    act_ref[...] = (_silu(outs[0]) * outs[1]).astype(act_ref.dtype)

    _sum_weighted_pages(job, step, pw_ref, v_refs, acc_ref)

    @pl.when(step == job.n_steps - 1)
    def _():
        oa_ref[...] = acc_ref[...].astype(oa_ref.dtype)


def _ffn_up_prompt(xn, w_up, cw, cb, w_s, cache_v, page_table, *, seq):
    m, k = xn.shape
    n2 = w_up.shape[1]
    dff = n2 // 2
    tm = min(FFN_ROW_TILE, m, seq)
    tn = COL_TILE
    nj = dff // tn
    ni = m // tm
    cwid = cw.shape[0]
    cb2 = cb.reshape(1, n2)
    job = _PageJob(cache_v, page_table, nj * ni)
    w_steps = job.rows_to_steps(w_s.reshape(job.total, job.width))

    def step_of(j, i):
        return j * ni + i

    act, ta, tb, o_a = pl.pallas_call(
        functools.partial(_ffn_up_prompt_kernel, seq=seq, conv_w=cwid, job=job),
        grid_spec=pltpu.PrefetchScalarGridSpec(
            num_scalar_prefetch=1,
            grid=(nj, ni),
            in_specs=[
                pl.BlockSpec((tm, k), lambda j, i, pt: (i, 0)),
                pl.BlockSpec((k, tn), lambda j, i, pt: (0, j)),
                pl.BlockSpec((k, tn), lambda j, i, pt: (0, nj + j)),
                pl.BlockSpec((cwid, tn), lambda j, i, pt: (0, j)),
                pl.BlockSpec((cwid, tn), lambda j, i, pt: (0, nj + j)),
                pl.BlockSpec((1, tn), lambda j, i, pt: (0, j)),
                pl.BlockSpec((1, tn), lambda j, i, pt: (0, nj + j)),
                pl.BlockSpec((None, job.pps_pad, job.width), lambda j, i, pt: (step_of(j, i), 0, 0)),
            ] + job.page_specs(step_of),
            out_specs=[
                pl.BlockSpec((tm, tn), lambda j, i, pt: (i, j)),
                pl.BlockSpec((None, SUBLANES, tn), lambda j, i, pt: (i, 0, j)),
                pl.BlockSpec((None, SUBLANES, tn), lambda j, i, pt: (i, 0, j)),
                pl.BlockSpec((job.db, job.heads, job.hd), lambda j, i, pt: (0, 0, 0)),
            ],
            scratch_shapes=[pltpu.VMEM((tm + SUBLANES, tn), F32), pltpu.VMEM((tm + SUBLANES, tn), F32),
                            pltpu.VMEM((job.db, job.heads, job.hd), F32)],
        ),
        out_shape=[
            jax.ShapeDtypeStruct((m, dff), BF16),
            jax.ShapeDtypeStruct((ni, SUBLANES, dff), F32),
            jax.ShapeDtypeStruct((ni, SUBLANES, dff), F32),
            jax.ShapeDtypeStruct((job.db, job.heads, job.hd), BF16),
        ],
        compiler_params=_cparams(("arbitrary", "arbitrary")),
        name="ffn_up_prompt",
    )(page_table.reshape(-1), xn, w_up, w_up, cw, cw, cb2, cb2, w_steps, *([cache_v] * job.pps))
    per_seq = seq // tm
    tails = jnp.concatenate([ta, tb], axis=-1)[per_seq - 1::per_seq, SUBLANES - (cwid - 1):, :]
    return act, tails, o_a.reshape(job.db, job.heads * job.hd)


def _ffn_up_sample_kernel(xn_ref, wa_ref, wb_ref, cwa_ref, cwb_ref, cba_ref, cbb_ref, sta_ref, stb_ref,
                          act_ref, ua_ref, ub_ref, *, conv_w):
    xn = xn_ref[...]
    outs = []
    for w_ref, cw_ref, cb_ref, st_ref, u_ref in ((wa_ref, cwa_ref, cba_ref, sta_ref, ua_ref),
                                                 (wb_ref, cwb_ref, cbb_ref, stb_ref, ub_ref)):
        up = _dot(xn, w_ref[...])
        u_ref[...] = up
        acc = cb_ref[...] + up * cw_ref[conv_w - 1:conv_w, :]
        for j in range(conv_w - 1):
            acc = acc + st_ref[j] * cw_ref[j:j + 1, :]
        outs.append(acc)
    act_ref[...] = (_silu(outs[0]) * outs[1]).astype(act_ref.dtype)


def _ffn_up_sample(xn, w_up, cw, cb, state):
    m, k = xn.shape
    n2 = w_up.shape[1]
    dff = n2 // 2
    tn = COL_TILE
    nj = dff // tn
    cwid = cw.shape[0]
    cb2 = cb.reshape(1, n2)
    st = state.transpose(1, 0, 2)
    act, ua, ub = pl.pallas_call(
        functools.partial(_ffn_up_sample_kernel, conv_w=cwid),
        grid=(nj,),
        in_specs=[
            pl.BlockSpec((m, k), lambda j: (0, 0)),
            pl.BlockSpec((k, tn), lambda j: (0, j)),
            pl.BlockSpec((k, tn), lambda j: (0, nj + j)),
            pl.BlockSpec((cwid, tn), lambda j: (0, j)),
            pl.BlockSpec((cwid, tn), lambda j: (0, nj + j)),
            pl.BlockSpec((1, tn), lambda j: (0, j)),
            pl.BlockSpec((1, tn), lambda j: (0, nj + j)),
            pl.BlockSpec((cwid - 1, m, tn), lambda j: (0, 0, j)),
            pl.BlockSpec((cwid - 1, m, tn), lambda j: (0, 0, nj + j)),
        ],
        out_specs=[pl.BlockSpec((m, tn), lambda j: (0, j))] * 3,
        out_shape=[
            jax.ShapeDtypeStruct((m, dff), BF16),
            jax.ShapeDtypeStruct((m, dff), F32),
            jax.ShapeDtypeStruct((m, dff), F32),
        ],
        compiler_params=_cparams(("parallel",)),
        name="ffn_up_sample",
    )(xn, w_up, w_up, cw, cw, cb2, cb2, st, st)
    up = jnp.concatenate([ua, ub], axis=-1)
    new_state = jnp.concatenate([state[:, 1:, :], up[:, None, :]], axis=1)
    return act, new_state


def _ple_kernel(x_ref, xt_ref, p_ref, g_ref, wp_ref, wg_ref, o_ref, xn_ref, *, row_chunk):
    @pl.when(pl.program_id(1) == 0)
    def _():
        tm = x_ref.shape[0]
        for c in range(tm // row_chunk):
            rows = pl.ds(c * row_chunk, row_chunk)
            xn_ref[rows, :] = _rms_rows(x_ref[rows, :], g_ref[...]).astype(BF16)

    emb = _dot(p_ref[...].astype(BF16), wp_ref[...])
    gate = _dot(xn_ref[...], wg_ref[...])
    o_ref[...] = xt_ref[...] + emb * jax.nn.sigmoid(gate)


def _ple(x, p, g, w_ple, w_gate):
    m, d = x.shape
    kp = p.shape[1]
    tm = min(ROW_TILE, m)
    tn = COL_TILE
    row_chunk = min(256, tm)
    return pl.pallas_call(
        functools.partial(_ple_kernel, row_chunk=row_chunk),
        grid=(m // tm, d // tn),
        in_specs=[
            pl.BlockSpec((tm, d), lambda i, j: (i, 0)),
            pl.BlockSpec((tm, tn), lambda i, j: (i, j)),
            pl.BlockSpec((tm, kp), lambda i, j: (i, 0)),
            pl.BlockSpec((1, d), lambda i, j: (0, 0)),
            pl.BlockSpec((kp, tn), lambda i, j: (0, j)),
            pl.BlockSpec((d, tn), lambda i, j: (0, j)),
        ],
        out_specs=pl.BlockSpec((tm, tn), lambda i, j: (i, j)),
        out_shape=jax.ShapeDtypeStruct((m, d), F32),
        scratch_shapes=[pltpu.VMEM((tm, d), BF16)],
        compiler_params=_cparams(("parallel", "arbitrary")),
        name="ple",
    )(x, x, p, g.reshape(1, d), w_ple, w_gate)


def kernel(x_prompt, x_sample, cache_k, cache_v, state_ssm, state_conv, state_ffn_conv, page_table,
           p_prompt, p_sample, g_mix, w_in, g_q, g_k, sb_bias, conv_w, conv_b, dt_bias, a_log, d_skip, g_ssm,
           w_proj_a, w_proj_b, w_out, g_ffn, w_up, ffn_conv_w, ffn_conv_b, w_down, w_ple, g_ple, w_ple_gate):
    depth = w_in.shape[0]
    assert depth == 1
    batch, seq, d_model = x_prompt.shape
    db, dseq, _ = x_sample.shape
    assert dseq == 1
    _, n_pool, page, sb_heads, hd = cache_k.shape
    sb_width = sb_heads * hd
    _, _, heads, pdim, nstate = state_ssm.shape
    inner = heads * pdim
    conv_dim = state_conv.shape[-1]
    groups = (conv_dim - inner) // (2 * nstate)
    dims = (heads, pdim, nstate, groups)

    off_z = 3 * sb_width
    off_dt = off_z + inner + conv_dim
    off_ga = off_dt + heads
    w_in0 = w_in[0]
    w_tail = jnp.concatenate(
        [w_in0[:, off_dt:off_ga], jnp.zeros((d_model, COL_TILE - heads), F32), w_in0[:, off_ga:]], axis=1)
    tail_ga, tail_gb = COL_TILE, COL_TILE + d_model
    wup_b = w_up[0].astype(BF16)
    wa_b = w_proj_a[0].astype(BF16)
    wb_b = w_proj_b[0].astype(BF16)
    wout_b = w_out[0].astype(BF16)
    wdown_b = w_down[0].astype(BF16)
    wple_b = w_ple[0].astype(BF16)
    wgate_b = w_ple_gate[0].astype(BF16)
    gq = g_q[0].reshape(1, hd)
    gk = g_k[0].reshape(1, hd)

    def mixer_inputs(x2):
        xn = _rms_norm_bf16(x2, g_mix[0])
        proj = functools.partial(_proj, xn, w_in0, tn=PROJ_COL_TILE, hd=hd)
        (qn,) = proj(col0=0, ncols=sb_width, epilogue=functools.partial(_epi_q, hd=hd), extras=[gq],
                     out_kinds=["bf16"], name="proj_q")
        kn, knb = proj(col0=sb_width, ncols=sb_width, epilogue=functools.partial(_epi_k, hd=hd), extras=[gk],
                       out_kinds=["heads", "bf16"], name="proj_k")
        v, vbt = proj(col0=2 * sb_width, ncols=sb_width, epilogue=functools.partial(_epi_v, hd=hd), extras=[],
                      out_kinds=["heads", "bf16_t"], name="proj_v")
        (zx,) = proj(col0=off_z, ncols=inner + conv_dim, epilogue=_epi_plain, extras=[],
                     out_kinds=["f32"], name="proj_zx")
        (tail,) = _proj(xn, w_tail, col0=0, ncols=w_tail.shape[1], tn=COL_TILE, epilogue=_epi_plain, extras=[],
                        out_kinds=["f32"], hd=hd, name="proj_tail")
        return qn, kn, knb, v, vbt, zx, tail

    def merge_branches(x2, o_a, o_b, tail):
        merged = _merge(o_a, o_b, wa_b, wb_b, tail, off_ga=tail_ga, off_gb=tail_gb)
        return _resid_matmul_norm(x2, merged, wout_b, g_ffn[0])

    def ffn_down_and_embed(x1, act, p2):
        return _ple(_resid_matmul(x1, act, wdown_b), p2, g_ple[0], wple_b, wgate_b)

    xp2 = x_prompt.reshape(batch * seq, d_model)
    xs2 = x_sample.reshape(db, d_model)
    qn, kn, knb, v, vbt, zx, tail = mixer_inputs(xp2)
    qn_s, kn_s, _, v_s, _, zx_s, tail_s = mixer_inputs(xs2)
    n_pages = page_table.shape[1]

    o_a = _sb_prompt(qn, knb, vbt, sb_bias[0], batch=batch, seq=seq, heads=sb_heads, hd=hd)
    o_b, ssm_p, z_s = _ssd_prompt(zx, tail[:, :heads], conv_w[0], conv_b[0], dt_bias[0], a_log[0], d_skip[0],
                                  g_ssm[0], qn_s, cache_k, page_table, batch=batch, seq=seq, dims=dims)
    z_s = z_s.reshape(db, n_pages, page, sb_heads).transpose(0, 1, 3, 2)
    w_s = _sb_weights(z_s, sb_bias[0])
    w_s = w_s.transpose(0, 1, 3, 2).reshape(db, n_pages, page * sb_heads)
    cw = conv_w.shape[1]
    conv_p = zx.reshape(batch, seq, -1)[:, seq - (cw - 1):, inner:]
    x1, xn1 = merge_branches(xp2, o_a, o_b, tail)
    act, ffn_p, o_a_s = _ffn_up_prompt(xn1, wup_b, ffn_conv_w[0], ffn_conv_b[0], w_s, cache_v, page_table, seq=seq)
    yp = ffn_down_and_embed(x1, act, p_prompt[0].reshape(batch * seq, -1))
    k_p = kn.reshape(batch, seq, sb_heads, hd)
    v_p = v.reshape(batch, seq, sb_heads, hd)

    o_b_s, conv_s, ssm_s = _ssd_sample(
        zx_s[:, inner:], zx_s[:, :inner], tail_s[:, :heads], state_conv[0], state_ssm[0],
        conv_w[0], conv_b[0], dt_bias[0], a_log[0], d_skip[0], g_ssm[0], dims=dims)
    x1_s, xn1_s = merge_branches(xs2, o_a_s, o_b_s, tail_s)
    act_s, ffn_s = _ffn_up_sample(xn1_s, wup_b, ffn_conv_w[0], ffn_conv_b[0], state_ffn_conv[0])
    ys = ffn_down_and_embed(x1_s, act_s, p_sample[0].reshape(db, -1))

    return (yp.reshape(batch, seq, d_model), ys.reshape(db, 1, d_model),
            k_p[None], v_p[None], ssm_p[None], conv_p[None], ffn_p[None],
            kn_s.reshape(db, 1, sb_heads, hd)[None], v_s.reshape(db, 1, sb_heads, hd)[None],
            ssm_s[None], conv_s[None], ffn_s[None])
```

```python
import functools
import math

import jax
import jax.numpy as jnp
from jax import lax
from jax.experimental import pallas as pl
from jax.experimental.pallas import tpu as pltpu

F32 = jnp.float32
BF16 = jnp.bfloat16
EPS = 1e-6
LANES = 128
SUBLANES = 8
VMEM_LIMIT = 56 * 1024 * 1024
COL_TILE = 512
ROW_TILE = 1024
PROJ_COL_TILE = 1024
ATTN_BLOCK = 256
ATTN_HEADS_PER_STEP = 4
FFN_ROW_TILE = 512


def _cparams(sem):
    return pltpu.CompilerParams(dimension_semantics=sem, vmem_limit_bytes=VMEM_LIMIT)


def _dot(a, b):
    return jnp.dot(a, b, preferred_element_type=F32)


def _dot_nt(a, b):
    return lax.dot_general(a, b, (((1,), (1,)), ((), ())), preferred_element_type=F32)


def _split3(x):
    hi = x.astype(BF16)
    r = x - hi.astype(F32)
    mid = r.astype(BF16)
    lo = (r - mid.astype(F32)).astype(BF16)
    return hi, mid, lo


def _dot3(x, m):
    hi, mid, lo = _split3(x)
    return _dot(hi, m) + _dot(mid, m) + _dot(lo, m)


def _softplus(x):
    return jnp.maximum(x, 0.0) + jnp.log(1.0 + jnp.exp(-jnp.abs(x)))


LOG2E = math.log2(math.e)


def _softplus2(x):
    neg_abs = lax.bitcast_convert_type(lax.bitcast_convert_type(x, jnp.uint32) | jnp.uint32(0x80000000), F32)
    return jnp.maximum(x, 0.0) + jnp.log2(1.0 + jnp.exp2(neg_abs))


def _silu(x):
    return x * jax.nn.sigmoid(x)


def _rms_rows(x, g):
    ms = jnp.mean(x * x, axis=-1, keepdims=True)
    return x * lax.rsqrt(ms + EPS) * g


def _rms_norm_kernel(x_ref, g_ref, o_ref):
    o_ref[...] = _rms_rows(x_ref[...], g_ref[...]).astype(BF16)


def _rms_norm_bf16(x, g):
    m, k = x.shape
    tm = min(256, m)
    return pl.pallas_call(
        _rms_norm_kernel,
        grid=(m // tm,),
        in_specs=[pl.BlockSpec((tm, k), lambda i: (i, 0)), pl.BlockSpec((1, k), lambda i: (0, 0))],
        out_specs=pl.BlockSpec((tm, k), lambda i: (i, 0)),
        out_shape=jax.ShapeDtypeStruct((m, k), BF16),
        compiler_params=_cparams(("parallel",)),
        name="rms_norm",
    )(x, g.reshape(1, k))


def _proj_kernel(x_ref, w_ref, *rest, n_extra, n_out, epilogue):
    extras = rest[:n_extra]
    outs = rest[n_extra:n_extra + n_out]
    wb_ref = rest[n_extra + n_out]

    @pl.when(pl.program_id(1) == 0)
    def _():
        wb_ref[...] = w_ref[...].astype(BF16)

    epilogue(_dot(x_ref[...], wb_ref[...]), *extras, *outs)


def _epi_plain(acc, o_ref):
    o_ref[...] = acc


def _epi_q(acc, g_ref, qn_ref, *, hd):
    scale = LOG2E / math.sqrt(hd)
    for h in range(acc.shape[1] // hd):
        lanes = slice(h * hd, (h + 1) * hd)
        qn_ref[:, lanes] = (_rms_rows(acc[:, lanes], g_ref[...]) * scale).astype(BF16)


def _epi_k(acc, g_ref, kn_ref, knb_ref, *, hd):
    for h in range(acc.shape[1] // hd):
        lanes = slice(h * hd, (h + 1) * hd)
        kn = _rms_rows(acc[:, lanes], g_ref[...])
        kn_ref[:, h, :] = kn
        knb_ref[:, lanes] = kn.astype(BF16)


def _epi_v(acc, v_ref, vbt_ref, *, hd):
    for h in range(acc.shape[1] // hd):
        lanes = slice(h * hd, (h + 1) * hd)
        v_ref[:, h, :] = acc[:, lanes]
        vbt_ref[lanes, :] = acc[:, lanes].T.astype(BF16)


def _proj(xn, w, *, col0, ncols, tn, epilogue, extras, out_kinds, hd, name):
    m, k = xn.shape
    tm = min(ROW_TILE, m)
    tn = min(tn, ncols)
    assert col0 % tn == 0 and ncols % tn == 0
    c0 = col0 // tn
    out_specs, out_shapes = [], []
    for kind in out_kinds:
        if kind == "heads":
            out_specs.append(pl.BlockSpec((tm, tn // hd, hd), lambda j, i: (i, j, 0)))
            out_shapes.append(jax.ShapeDtypeStruct((m, ncols // hd, hd), F32))
        elif kind == "bf16_t":
            out_specs.append(pl.BlockSpec((tn, tm), lambda j, i: (j, i)))
            out_shapes.append(jax.ShapeDtypeStruct((ncols, m), BF16))
        else:
            out_specs.append(pl.BlockSpec((tm, tn), lambda j, i: (i, j)))
            out_shapes.append(jax.ShapeDtypeStruct((m, ncols), F32 if kind == "f32" else BF16))
    extra_specs = [pl.BlockSpec(e.shape, lambda j, i: (0, 0)) for e in extras]
    return pl.pallas_call(
        functools.partial(_proj_kernel, n_extra=len(extras), n_out=len(out_kinds), epilogue=epilogue),
        grid=(ncols // tn, m // tm),
        in_specs=[pl.BlockSpec((tm, k), lambda j, i: (i, 0)),
                  pl.BlockSpec((k, tn), lambda j, i: (0, c0 + j))] + extra_specs,
        out_specs=out_specs,
        out_shape=out_shapes,
        scratch_shapes=[pltpu.VMEM((k, tn), BF16)],
        compiler_params=_cparams(("parallel", "arbitrary")),
        name=name,
    )(xn, w, *extras)


def _sb_prompt_kernel(bias_ref, q_ref, k_ref, vt_ref, tri_ref, o_ref, *, tq, tk, hd, hps):
    hb = pl.program_id(1)
    qi = pl.program_id(2)
    tri = tri_ref[...]

    def block(k0, state, masked):
        if masked:
            row = lax.broadcasted_iota(jnp.int32, (tq, tk), 0)
            col = lax.broadcasted_iota(jnp.int32, (tq, tk), 1)
            keep = col < row
        heads = range(hps)
        lanes = [slice(g * hd, (g + 1) * hd) for g in heads]
        zs = [_dot_nt(q_ref[:, lanes[g]], k_ref[pl.ds(k0, tk), lanes[g]]) + bias_ref[hb * hps + g] for g in heads]
        sps = [_softplus2(z) for z in zs]
        if masked:
            sps = [jnp.where(keep, sp, 0.0) for sp in sps]
        spb = [sp.astype(BF16) for sp in sps]
        tails = [_dot(sb, tri) for sb in spb]
        ws = [jnp.exp2(zs[g] - sps[g] - tails[g] - state[g][0]) for g in heads]
        if masked:
            ws = [jnp.where(keep, w, 0.0) for w in ws]
        pvs = [_dot_nt(vt_ref[lanes[g], pl.ds(k0, tk)], ws[g].astype(BF16)) for g in heads]
        sums = [tails[g][:, 0:1] + spb[g][:, 0:1].astype(F32) for g in heads]
        return tuple((state[g][0] + sums[g], state[g][1] + pvs[g]) for g in heads)

    state = tuple((jnp.zeros((tq, 1), F32), jnp.zeros((hd, tq), F32)) for _ in range(hps))
    state = block(pl.multiple_of(qi * tq, tq), state, True)

    def body(it, st):
        return block(pl.multiple_of((qi - 1 - it) * tk, tk), st, False)

    state = lax.fori_loop(0, qi, body, state)
    for g in range(hps):
        o_ref[:, g * hd:(g + 1) * hd] = state[g][1].T.astype(o_ref.dtype)


def _sb_prompt(qn, knb, vbt, sb_bias, *, batch, seq, heads, hd):
    tq = tk = min(ATTN_BLOCK, seq)
    hps = ATTN_HEADS_PER_STEP
    assert seq % tq == 0 and heads % hps == 0
    width = heads * hd
    q3 = qn.reshape(batch, seq, width)
    k3 = knb.reshape(batch, seq, width)
    tri = (jnp.arange(tk)[:, None] > jnp.arange(tk)[None, :]).astype(BF16)
    out = pl.pallas_call(
        functools.partial(_sb_prompt_kernel, tq=tq, tk=tk, hd=hd, hps=hps),
        grid=(batch, heads // hps, seq // tq),
        in_specs=[
            pl.BlockSpec(memory_space=pltpu.SMEM),
            pl.BlockSpec((None, tq, hps * hd), lambda b, h, i: (b, i, h)),
            pl.BlockSpec((None, seq, hps * hd), lambda b, h, i: (b, 0, h)),
            pl.BlockSpec((hps * hd, seq), lambda b, h, i: (h, b)),
            pl.BlockSpec((tk, tk), lambda b, h, i: (0, 0)),
        ],
        out_specs=pl.BlockSpec((None, tq, hps * hd), lambda b, h, i: (b, i, h)),
        out_shape=jax.ShapeDtypeStruct((batch, seq, width), BF16),
        compiler_params=_cparams(("parallel", "parallel", "arbitrary")),
        name="sb_attn_prompt",
    )(sb_bias.astype(F32) * LOG2E, q3, k3, vbt, tri)
    return out.reshape(batch * seq, width)


class _PageJob:
    def __init__(self, cache, page_table, n_steps):
        self.db, self.n_pages = page_table.shape
        _, _, self.page, self.heads, self.hd = cache.shape
        self.total = self.db * self.n_pages
        self.n_steps = n_steps
        self.pps = -(-self.total // n_steps)
        self.pps_pad = -(-self.pps // SUBLANES) * SUBLANES
        self.width = self.page * self.heads

    def page_specs(self, step_of):
        def spec(p):
            def index(*args):
                pt = args[-1]
                flat = jnp.minimum(step_of(*args[:-1]) * self.pps + p, self.total - 1)
                return (0, pt[flat], 0, 0, 0)
            return pl.BlockSpec((None, None, self.page, self.heads, self.hd), index)
        return [spec(p) for p in range(self.pps)]

    def sequence_of(self, step, p):
        return jnp.minimum(step * self.pps + p, self.total - 1) // self.n_pages

    def rows_to_steps(self, rows):
        rows = jnp.pad(rows, ((0, self.n_steps * self.pps - self.total), (0, 0)))
        rows = rows.reshape(self.n_steps, self.pps, self.width)
        return jnp.pad(rows, ((0, 0), (0, self.pps_pad - self.pps), (0, 0)))

    def steps_to_rows(self, blocks):
        return blocks[:, :self.pps].reshape(self.n_steps * self.pps, self.width)[:self.total]


def _own_head(heads, width):
    row = lax.broadcasted_iota(jnp.int32, (heads, width), 0)
    col = lax.broadcasted_iota(jnp.int32, (heads, width), 1)
    return col % heads == row


def _flat_page(page_ref, heads):
    kp = page_ref[...]
    return kp.reshape(kp.shape[0] * heads, kp.shape[2]).astype(BF16)


def _score_page(job, step, p, q_ref, k_refs):
    q = q_ref[job.sequence_of(step, p)]
    full = _dot_nt(q, _flat_page(k_refs[p], job.heads))
    return jnp.sum(jnp.where(_own_head(job.heads, job.width), full, 0.0), axis=0, keepdims=True)


def _store_scores(job, rows, z_ref):
    if job.pps_pad > job.pps:
        rows = rows + [jnp.zeros((job.pps_pad - job.pps, job.width), F32)]
    z_ref[...] = jnp.concatenate(rows, axis=0)


def _sum_weighted_pages(job, step, w_ref, v_refs, acc_ref):
    own = _own_head(job.heads, job.width)
    w_all = w_ref[...]
    for p in range(job.pps):
        wm = jnp.where(own, jnp.broadcast_to(w_all[p:p + 1, :], (job.heads, job.width)), 0.0).astype(BF16)
        b = job.sequence_of(step, p)
        acc_ref[b] = acc_ref[b] + _dot(wm, _flat_page(v_refs[p], job.heads))


def _sb_weights_kernel(z_ref, bias_ref, w_ref, *, heads):
    n_pages, width = z_ref.shape
    lane = lax.broadcasted_iota(jnp.int32, (n_pages, width), 1)
    row = lax.broadcasted_iota(jnp.int32, (n_pages, width), 0)
    z = z_ref[...] + bias_ref[...]
    sp = _softplus2(z)

    incl = sp
    shift = heads
    while shift < width:
        moved = pltpu.roll(incl, width - shift, 1)
        incl = incl + jnp.where(lane < width - shift, moved, 0.0)
        shift *= 2
    total = jnp.where(lane < heads, incl, 0.0)
    shift = heads
    while shift < width:
        total = total + pltpu.roll(total, shift, 1)
        shift *= 2
    below = total
    shift = 1
    while shift < n_pages:
        moved = pltpu.roll(below, n_pages - shift, 0)
        below = below + jnp.where(row < n_pages - shift, moved, 0.0)
        shift *= 2
    later = (incl - sp) + (below - total)
    w_ref[...] = jnp.exp2(z - sp - later)


def _sb_weights(z, sb_bias):
    db, n_pages, width = z.shape
    heads = sb_bias.shape[0]
    assert width % heads == 0 and (width // heads) & (width // heads - 1) == 0 and n_pages & (n_pages - 1) == 0
    bias = jnp.tile(sb_bias.astype(F32) * LOG2E, width // heads).reshape(1, width)
    return pl.pallas_call(
        functools.partial(_sb_weights_kernel, heads=heads),
        grid=(db,),
        in_specs=[
            pl.BlockSpec((None, n_pages, width), lambda b: (b, 0, 0)),
            pl.BlockSpec((1, width), lambda b: (0, 0)),
        ],
        out_specs=pl.BlockSpec((None, n_pages, width), lambda b: (b, 0, 0)),
        out_shape=jax.ShapeDtypeStruct((db, n_pages, width), F32),
        compiler_params=_cparams(("parallel",)),
        name="sb_sample_weights",
    )(z, bias)


def _ssd_prompt_kernel(pt_ref, x_ref, bm_ref, cm_ref, z_ref, dt_ref,
                       wx_ref, wb_ref, wc_ref, bx_ref, bb_ref, bc_ref,
                       dtb_ref, al_ref, dskip_ref, gssm_ref, triu_ref, q_ref, *rest,
                       chunk, hpg, pdim, nstate, conv_w, job):
    k_refs = rest[:job.pps]
    o_ref, st_ref, sc_ref, ext_ref, ht_ref = rest[job.pps:]
    c = pl.program_id(2)
    gw = hpg * pdim
    halo = SUBLANES
    widths = (gw, nstate, nstate)
    offs = (0, gw, gw + nstate)

    @pl.when(c == 0)
    def _():
        ext_ref[0:halo, :] = jnp.zeros((halo, ext_ref.shape[1]), F32)
        ht_ref[...] = jnp.zeros_like(ht_ref)

    raws = (x_ref, bm_ref, cm_ref)
    wrefs = (wx_ref, wb_ref, wc_ref)
    brefs = (bx_ref, bb_ref, bc_ref)
    conv = []
    for raw, wref, bref, off, wd in zip(raws, wrefs, brefs, offs, widths):
        ext_ref[halo:halo + chunk, off:off + wd] = raw[...]
        acc = jnp.broadcast_to(bref[...], (chunk, wd))
        for j in range(conv_w):
            start = halo - (conv_w - 1) + j
            acc = acc + ext_ref[start:start + chunk, off:off + wd] * wref[j:j + 1, :]
        conv.append(_silu(acc))
    ext_ref[0:halo, :] = ext_ref[chunk:chunk + halo, :]
    xs, bmat, cmat = conv

    dt_r = _softplus(dt_ref[...] + dtb_ref[...])
    a_r = -jnp.exp(al_ref[...])
    da_r = _dot3(dt_r * a_r, triu_ref[...])
    dt_c = dt_r.T
    da_c = da_r.T

    bmat_b = bmat.astype(BF16)
    cmat_b = cmat.astype(BF16)
    row = lax.broadcasted_iota(jnp.int32, (chunk, chunk), 0)
    col = lax.broadcasted_iota(jnp.int32, (chunk, chunk), 1)
    causal = col <= row
    cb = jnp.where(causal, _dot_nt(cmat_b, bmat_b), 0.0)
    bmat_t = bmat.T.astype(BF16)
    first = lax.broadcasted_iota(jnp.int32, (chunk, LANES), 1) < pdim

    heads_per_tile = LANES // pdim
    assert heads_per_tile == 2
    n_tiles = hpg // heads_per_tile
    step = (pl.program_id(0) * pl.num_programs(1) + pl.program_id(1)) * pl.num_programs(2) + c
    scores = []
    us = []
    ssq = jnp.zeros((chunk, 1), F32)
    for t in range(n_tiles):
        for p in range(t * job.pps // n_tiles, (t + 1) * job.pps // n_tiles):
            scores.append(_score_page(job, step, p, q_ref, k_refs))
        r0 = t * heads_per_tile
        lanes = slice(t * LANES, (t + 1) * LANES)
        da_bc = [jnp.broadcast_to(da_c[:, r:r + 1], (chunk, LANES)) for r in (r0, r0 + 1)]
        dt_bc = [jnp.broadcast_to(dt_c[:, r:r + 1], (chunk, LANES)) for r in (r0, r0 + 1)]
        da_pair = jnp.where(first, da_bc[0], da_bc[1])
        dt_pair = jnp.where(first, dt_bc[0], dt_bc[1])
        last_pair = da_pair[chunk - 1:chunk, :]
        decay = []
        for i, r in enumerate((r0, r0 + 1)):
            diff = da_bc[i] - da_r[r:r + 1, :]
            decay.append((cb * jnp.exp(jnp.minimum(diff, 0.0))).astype(BF16))
        m_cat = jnp.concatenate(decay, axis=1)
        xp = xs[:, lanes]
        xdt = xp * dt_pair
        xbd = jnp.concatenate([jnp.where(first, xdt, 0.0), jnp.where(first, 0.0, xdt)], axis=0).astype(BF16)
        y = _dot(m_cat, xbd)
        ht_old = ht_ref[:, lanes]
        y = y + _dot(cmat_b, ht_old.astype(BF16)) * jnp.exp(da_pair)
        xw = (xdt * jnp.exp(last_pair - da_pair)).astype(BF16)
        ht_ref[:, lanes] = ht_old * jnp.exp(last_pair) + _dot(bmat_t, xw)
        y = y + xp * dskip_ref[:, lanes]
        u = y * _silu(z_ref[:, lanes])
        ssq = ssq + jnp.sum(u * u, axis=1, keepdims=True)
        us.append(u)
    inv = lax.rsqrt(ssq / gw + EPS)
    for t, u in enumerate(us):
        lanes = slice(t * LANES, (t + 1) * LANES)
        o_ref[:, lanes] = (u * inv * gssm_ref[:, lanes]).astype(o_ref.dtype)
    _store_scores(job, scores, sc_ref)

    @pl.when(c == pl.num_programs(2) - 1)
    def _():
        st_ref[...] = ht_ref[...]


def _ssd_prompt(zx, dt_raw, conv_w, conv_b, dt_bias, a_log, d_skip, g_ssm, qn_s, cache_k, page_table, *,
                batch, seq, dims):
    heads, pdim, nstate, groups = dims
    hpg = heads // groups
    gw = hpg * pdim
    inner = heads * pdim
    chunk = min(LANES, seq)
    n_chunks = seq // chunk
    cw = conv_w.shape[0]
    zx3 = zx.reshape(batch, seq, zx.shape[1])
    dtt = dt_raw.T
    idx = jnp.arange(chunk)
    triu = (idx[:, None] <= idx[None, :]).astype(BF16)
    job = _PageJob(cache_k, page_table, batch * groups * n_chunks)

    zcol0 = 0
    xcol0 = inner // gw
    bcol0 = (2 * inner) // nstate
    ccol0 = (2 * inner + groups * nstate) // nstate
    wb0 = inner // nstate
    wc0 = (inner + groups * nstate) // nstate

    def step_of(b, g, c):
        return (b * groups + g) * n_chunks + c

    def tok(width, col0):
        return pl.BlockSpec((None, chunk, width), lambda b, g, c, pt: (b, c, col0 + g))

    def par(rows, width, col0):
        return pl.BlockSpec((rows, width), lambda b, g, c, pt: (0, col0 + g))

    def grp(shape):
        return pl.BlockSpec((None,) + shape, lambda b, g, c, pt: (g, 0, 0))

    conv_b2 = conv_b.reshape(1, -1)
    o_b, st, scores = pl.pallas_call(
        functools.partial(_ssd_prompt_kernel, chunk=chunk, hpg=hpg, pdim=pdim, nstate=nstate, conv_w=cw, job=job),
        grid_spec=pltpu.PrefetchScalarGridSpec(
            num_scalar_prefetch=1,
            grid=(batch, groups, n_chunks),
            in_specs=[
                tok(gw, xcol0), tok(nstate, bcol0), tok(nstate, ccol0), tok(gw, zcol0),
                pl.BlockSpec((hpg, chunk), lambda b, g, c, pt: (g, b * n_chunks + c)),
                par(cw, gw, 0), par(cw, nstate, wb0), par(cw, nstate, wc0),
                par(1, gw, 0), par(1, nstate, wb0), par(1, nstate, wc0),
                grp((hpg, 1)), grp((hpg, 1)),
                par(1, gw, 0), par(1, gw, 0),
                pl.BlockSpec((chunk, chunk), lambda b, g, c, pt: (0, 0)),
                pl.BlockSpec((job.db, job.heads, job.hd), lambda b, g, c, pt: (0, 0, 0)),
            ] + job.page_specs(step_of),
            out_specs=[
                pl.BlockSpec((None, chunk, gw), lambda b, g, c, pt: (b, c, g)),
                pl.BlockSpec((None, None, nstate, gw), lambda b, g, c, pt: (b, g, 0, 0)),
                pl.BlockSpec((None, job.pps_pad, job.width), lambda b, g, c, pt: (step_of(b, g, c), 0, 0)),
            ],
            scratch_shapes=[
                pltpu.VMEM((chunk + SUBLANES, gw + 2 * nstate), F32),
                pltpu.VMEM((nstate, gw), F32),
            ],
        ),
        out_shape=[
            jax.ShapeDtypeStruct((batch, seq, inner), BF16),
            jax.ShapeDtypeStruct((batch, groups, nstate, gw), F32),
            jax.ShapeDtypeStruct((job.n_steps, job.pps_pad, job.width), F32),
        ],
        compiler_params=_cparams(("parallel", "parallel", "arbitrary")),
        name="ssd_prompt",
    )(page_table.reshape(-1), zx3, zx3, zx3, zx3, dtt,
      conv_w, conv_w, conv_w, conv_b2, conv_b2, conv_b2,
      dt_bias.reshape(groups, hpg, 1), a_log.reshape(groups, hpg, 1),
      jnp.repeat(d_skip, pdim).reshape(1, inner), g_ssm.reshape(1, inner),
      triu, qn_s.reshape(job.db, job.heads, job.hd), *([cache_k] * job.pps))
    st = st.reshape(batch, groups, nstate, hpg, pdim).transpose(0, 1, 3, 4, 2)
    scores = job.steps_to_rows(scores).reshape(job.db, job.n_pages, job.width)
    return o_b.reshape(batch * seq, inner), st.reshape(batch, heads, pdim, nstate), scores


def _ssd_sample_kernel(xbc_ref, z_ref, dt_ref, cst_ref, h0_ref,
                       cw_ref, cb_ref, dtb_ref, al_ref, dskip_ref, gssm_ref, expand_ref,
                       o_ref, cnew_ref, h_ref, *, inner, nstate, groups, conv_w):
    gw = inner // groups
    x = xbc_ref[...]
    acc = cb_ref[...] + x * cw_ref[conv_w - 1:conv_w, :]
    for j in range(conv_w - 1):
        acc = acc + cst_ref[j:j + 1, :] * cw_ref[j:j + 1, :]
        if j > 0:
            cnew_ref[j - 1:j, :] = cst_ref[j:j + 1, :]
    cnew_ref[conv_w - 2:conv_w - 1, :] = x
    xc = _silu(acc)
    xs = xc[:, :inner]
    bv = xc[:, inner:inner + groups * nstate]
    cv = xc[:, inner + groups * nstate:]

    dt8 = jnp.broadcast_to(dt_ref[...], (SUBLANES, dt_ref.shape[1]))
    dt = _softplus(_dot3(dt8, expand_ref[...])[0:1, :] + dtb_ref[...])
    dec = jnp.exp(dt * (-jnp.exp(al_ref[...])))
    xdt = xs * dt
    row8 = lax.broadcasted_iota(jnp.int32, (SUBLANES, gw), 0)
    ys = []
    for g in range(groups):
        lanes = slice(g * gw, (g + 1) * gw)
        glanes = slice(g * nstate, (g + 1) * nstate)
        stacked = jnp.where(row8 == 0, xdt[:, lanes], jnp.where(row8 == 1, dec[:, lanes], 0.0))
        cols = stacked.T
        hn = h0_ref[lanes, :] * cols[:, 1:2] + cols[:, 0:1] * bv[:, glanes]
        h_ref[lanes, :] = hn
        cg = jnp.broadcast_to(cv[:, glanes], (SUBLANES, nstate)).astype(BF16)
        ys.append(_dot_nt(cg, hn.astype(BF16))[0:1, :])
    y = jnp.concatenate(ys, axis=1) + xs * dskip_ref[...]
    u = y * _silu(z_ref[...])
    for g in range(groups):
        lanes = slice(g * gw, (g + 1) * gw)
        ug = u[:, lanes]
        inv = lax.rsqrt(jnp.mean(ug * ug, axis=1, keepdims=True) + EPS)
        o_ref[:, lanes] = (ug * inv * gssm_ref[:, lanes]).astype(o_ref.dtype)


def _ssd_sample(xbc, z, dt_raw, state_conv, state_ssm, conv_w, conv_b, dt_bias, a_log, d_skip, g_ssm, *, dims):
    heads, pdim, nstate, groups = dims
    inner = heads * pdim
    db, conv_dim = xbc.shape
    cw = conv_w.shape[0]
    rep = lambda v: jnp.repeat(v.astype(F32), pdim).reshape(1, inner)
    expand = (jnp.arange(heads)[:, None] == (jnp.arange(inner)[None, :] // pdim)).astype(BF16)
    row = lambda width: pl.BlockSpec((None, 1, width), lambda b: (b, 0, 0))
    full = lambda shape: pl.BlockSpec(shape, lambda b: (0,) * len(shape))
    o_b, conv_new, h_new = pl.pallas_call(
        functools.partial(_ssd_sample_kernel, inner=inner, nstate=nstate, groups=groups, conv_w=cw),
        grid=(db,),
        in_specs=[
            row(conv_dim), row(inner), row(heads),
            pl.BlockSpec((None, cw - 1, conv_dim), lambda b: (b, 0, 0)),
            pl.BlockSpec((None, inner, nstate), lambda b: (b, 0, 0)),
            full((cw, conv_dim)), full((1, conv_dim)),
            full((1, inner)), full((1, inner)), full((1, inner)), full((1, inner)),
            full((heads, inner)),
        ],
        out_specs=[
            row(inner),
            pl.BlockSpec((None, cw - 1, conv_dim), lambda b: (b, 0, 0)),
            pl.BlockSpec((None, inner, nstate), lambda b: (b, 0, 0)),
        ],
        out_shape=[
            jax.ShapeDtypeStruct((db, 1, inner), BF16),
            jax.ShapeDtypeStruct((db, cw - 1, conv_dim), F32),
            jax.ShapeDtypeStruct((db, inner, nstate), F32),
        ],
        compiler_params=_cparams(("parallel",)),
        name="ssd_sample",
    )(xbc.reshape(db, 1, conv_dim), z.reshape(db, 1, inner), dt_raw.reshape(db, 1, heads),
      state_conv, state_ssm.reshape(db, inner, nstate),
      conv_w, conv_b.reshape(1, conv_dim), rep(dt_bias), rep(a_log), rep(d_skip), g_ssm.reshape(1, inner),
      expand)
    return o_b.reshape(db, inner), conv_new, h_new.reshape(db, heads, pdim, nstate)


def _merge_kernel(oa_ref, ob_ref, wa_ref, wb_ref, ga_ref, gb_ref, o_ref):
    a = _dot(oa_ref[...], wa_ref[...])
    b = _dot(ob_ref[...], wb_ref[...])
    o_ref[...] = (jax.nn.sigmoid(ga_ref[...]) * a + jax.nn.sigmoid(gb_ref[...]) * b).astype(o_ref.dtype)


def _merge(o_a, o_b, wa, wb, gates, *, off_ga, off_gb):
    m, ka = o_a.shape
    kb = o_b.shape[1]
    n = wa.shape[1]
    tm = min(ROW_TILE, m)
    tn = COL_TILE
    ga0, gb0 = off_ga // tn, off_gb // tn
    return pl.pallas_call(
        _merge_kernel,
        grid=(m // tm, n // tn),
        in_specs=[
            pl.BlockSpec((tm, ka), lambda i, j: (i, 0)),
            pl.BlockSpec((tm, kb), lambda i, j: (i, 0)),
            pl.BlockSpec((ka, tn), lambda i, j: (0, j)),
            pl.BlockSpec((kb, tn), lambda i, j: (0, j)),
            pl.BlockSpec((tm, tn), lambda i, j: (i, ga0 + j)),
            pl.BlockSpec((tm, tn), lambda i, j: (i, gb0 + j)),
        ],
        out_specs=pl.BlockSpec((tm, tn), lambda i, j: (i, j)),
        out_shape=jax.ShapeDtypeStruct((m, n), BF16),
        compiler_params=_cparams(("parallel", "arbitrary")),
        name="merge",
    )(o_a, o_b, wa, wb, gates, gates)


def _resid_mm_kernel(x_ref, a_ref, w_ref, g_ref, o_ref, on_ref):
    y = x_ref[...] + _dot(a_ref[...], w_ref[...])
    o_ref[...] = y
    on_ref[...] = _rms_rows(y, g_ref[...]).astype(BF16)


def _resid_matmul_norm(x, a, w, g):
    m, n = x.shape
    k = a.shape[1]
    tm = min(256, m)
    return pl.pallas_call(
        _resid_mm_kernel,
        grid=(m // tm,),
        in_specs=[
            pl.BlockSpec((tm, n), lambda i: (i, 0)),
            pl.BlockSpec((tm, k), lambda i: (i, 0)),
            pl.BlockSpec((k, n), lambda i: (0, 0)),
            pl.BlockSpec((1, n), lambda i: (0, 0)),
        ],
        out_specs=[pl.BlockSpec((tm, n), lambda i: (i, 0)), pl.BlockSpec((tm, n), lambda i: (i, 0))],
        out_shape=[jax.ShapeDtypeStruct((m, n), F32), jax.ShapeDtypeStruct((m, n), BF16)],
        compiler_params=_cparams(("parallel",)),
        name="resid_matmul_norm",
    )(x, a, w, g.reshape(1, n))


def _resid_mm_tiled_kernel(x_ref, a_ref, w_ref, o_ref):
    o_ref[...] = x_ref[...] + _dot(a_ref[...], w_ref[...])


def _resid_matmul(x, a, w):
    m, n = x.shape
    k = a.shape[1]
    tm = min(ROW_TILE, m)
    tn = COL_TILE
    return pl.pallas_call(
        _resid_mm_tiled_kernel,
        grid=(m // tm, n // tn),
        in_specs=[
            pl.BlockSpec((tm, tn), lambda i, j: (i, j)),
            pl.BlockSpec((tm, k), lambda i, j: (i, 0)),
            pl.BlockSpec((k, tn), lambda i, j: (0, j)),
        ],
        out_specs=pl.BlockSpec((tm, tn), lambda i, j: (i, j)),
        out_shape=jax.ShapeDtypeStruct((m, n), F32),
        compiler_params=_cparams(("parallel", "arbitrary")),
        name="resid_matmul",
    )(x, a, w)


def _ffn_up_prompt_kernel(pt_ref, xn_ref, wa_ref, wb_ref, cwa_ref, cwb_ref, cba_ref, cbb_ref, pw_ref, *rest,
                          seq, conv_w, job):
    v_refs = rest[:job.pps]
    act_ref, ta_ref, tb_ref, oa_ref, exta_ref, extb_ref, acc_ref = rest[job.pps:]
    i = pl.program_id(1)
    step = pl.program_id(0) * pl.num_programs(1) + i
    tm = xn_ref.shape[0]
    halo = SUBLANES

    @pl.when(step == 0)
    def _():
        acc_ref[...] = jnp.zeros_like(acc_ref)

    xn = xn_ref[...]
    start_of_seq = (i * tm) % seq == 0
    outs = []
    for w_ref, cw_ref, cb_ref, ext_ref, tail_ref in ((wa_ref, cwa_ref, cba_ref, exta_ref, ta_ref),
                                                     (wb_ref, cwb_ref, cbb_ref, extb_ref, tb_ref)):
        up = _dot(xn, w_ref[...])
        history = ext_ref[0:halo, :]
        ext_ref[0:halo, :] = jnp.where(start_of_seq, jnp.zeros_like(history), history)
        ext_ref[halo:halo + tm, :] = up
        acc = jnp.broadcast_to(cb_ref[...], up.shape)
        for j in range(conv_w):
            start = halo - (conv_w - 1) + j
            acc = acc + ext_ref[start:start + tm, :] * cw_ref[j:j + 1, :]
        tail = ext_ref[tm:tm + halo, :]
        ext_ref[0:halo, :] = tail
        tail_ref[...] = tail
        outs.append(acc)
    act_ref[...] = (_silu(outs[0]) * outs[1]).astype(act_ref.dtype)

    _sum_weighted_pages(job, step, pw_ref, v_refs, acc_ref)

    @pl.when(step == job.n_steps - 1)
    def _():
        oa_ref[...] = acc_ref[...].astype(oa_ref.dtype)


def _ffn_up_prompt(xn, w_up, cw, cb, w_s, cache_v, page_table, *, seq):
    m, k = xn.shape
    n2 = w_up.shape[1]
    dff = n2 // 2
    tm = min(FFN_ROW_TILE, m, seq)
    tn = COL_TILE
    nj = dff // tn
    ni = m // tm
    cwid = cw.shape[0]
    cb2 = cb.reshape(1, n2)
    job = _PageJob(cache_v, page_table, nj * ni)
    w_steps = job.rows_to_steps(w_s.reshape(job.total, job.width))

    def step_of(j, i):
        return j * ni + i

    act, ta, tb, o_a = pl.pallas_call(
        functools.partial(_ffn_up_prompt_kernel, seq=seq, conv_w=cwid, job=job),
        grid_spec=pltpu.PrefetchScalarGridSpec(
            num_scalar_prefetch=1,
            grid=(nj, ni),
            in_specs=[
                pl.BlockSpec((tm, k), lambda j, i, pt: (i, 0)),
                pl.BlockSpec((k, tn), lambda j, i, pt: (0, j)),
                pl.BlockSpec((k, tn), lambda j, i, pt: (0, nj + j)),
                pl.BlockSpec((cwid, tn), lambda j, i, pt: (0, j)),
                pl.BlockSpec((cwid, tn), lambda j, i, pt: (0, nj + j)),
                pl.BlockSpec((1, tn), lambda j, i, pt: (0, j)),
                pl.BlockSpec((1, tn), lambda j, i, pt: (0, nj + j)),
                pl.BlockSpec((None, job.pps_pad, job.width), lambda j, i, pt: (step_of(j, i), 0, 0)),
            ] + job.page_specs(step_of),
            out_specs=[
                pl.BlockSpec((tm, tn), lambda j, i, pt: (i, j)),
                pl.BlockSpec((None, SUBLANES, tn), lambda j, i, pt: (i, 0, j)),
                pl.BlockSpec((None, SUBLANES, tn), lambda j, i, pt: (i, 0, j)),
                pl.BlockSpec((job.db, job.heads, job.hd), lambda j, i, pt: (0, 0, 0)),
            ],
            scratch_shapes=[pltpu.VMEM((tm + SUBLANES, tn), F32), pltpu.VMEM((tm + SUBLANES, tn), F32),
                            pltpu.VMEM((job.db, job.heads, job.hd), F32)],
        ),
        out_shape=[
            jax.ShapeDtypeStruct((m, dff), BF16),
            jax.ShapeDtypeStruct((ni, SUBLANES, dff), F32),
            jax.ShapeDtypeStruct((ni, SUBLANES, dff), F32),
            jax.ShapeDtypeStruct((job.db, job.heads, job.hd), BF16),
        ],
        compiler_params=_cparams(("arbitrary", "arbitrary")),
        name="ffn_up_prompt",
    )(page_table.reshape(-1), xn, w_up, w_up, cw, cw, cb2, cb2, w_steps, *([cache_v] * job.pps))
    per_seq = seq // tm
    tails = jnp.concatenate([ta, tb], axis=-1)[per_seq - 1::per_seq, SUBLANES - (cwid - 1):, :]
    return act, tails, o_a.reshape(job.db, job.heads * job.hd)


def _ffn_up_sample_kernel(xn_ref, wa_ref, wb_ref, cwa_ref, cwb_ref, cba_ref, cbb_ref, sta_ref, stb_ref,
                          act_ref, ua_ref, ub_ref, *, conv_w):
    xn = xn_ref[...]
    outs = []
    for w_ref, cw_ref, cb_ref, st_ref, u_ref in ((wa_ref, cwa_ref, cba_ref, sta_ref, ua_ref),
                                                 (wb_ref, cwb_ref, cbb_ref, stb_ref, ub_ref)):
        up = _dot(xn, w_ref[...])
        u_ref[...] = up
        acc = cb_ref[...] + up * cw_ref[conv_w - 1:conv_w, :]
        for j in range(conv_w - 1):
            acc = acc + st_ref[j] * cw_ref[j:j + 1, :]
        outs.append(acc)
    act_ref[...] = (_silu(outs[0]) * outs[1]).astype(act_ref.dtype)


def _ffn_up_sample(xn, w_up, cw, cb, state):
    m, k = xn.shape
    n2 = w_up.shape[1]
    dff = n2 // 2
    tn = COL_TILE
    nj = dff // tn
    cwid = cw.shape[0]
    cb2 = cb.reshape(1, n2)
    st = state.transpose(1, 0, 2)
    act, ua, ub = pl.pallas_call(
        functools.partial(_ffn_up_sample_kernel, conv_w=cwid),
        grid=(nj,),
        in_specs=[
            pl.BlockSpec((m, k), lambda j: (0, 0)),
            pl.BlockSpec((k, tn), lambda j: (0, j)),
            pl.BlockSpec((k, tn), lambda j: (0, nj + j)),
            pl.BlockSpec((cwid, tn), lambda j: (0, j)),
            pl.BlockSpec((cwid, tn), lambda j: (0, nj + j)),
            pl.BlockSpec((1, tn), lambda j: (0, j)),
            pl.BlockSpec((1, tn), lambda j: (0, nj + j)),
            pl.BlockSpec((cwid - 1, m, tn), lambda j: (0, 0, j)),
            pl.BlockSpec((cwid - 1, m, tn), lambda j: (0, 0, nj + j)),
        ],
        out_specs=[pl.BlockSpec((m, tn), lambda j: (0, j))] * 3,
        out_shape=[
            jax.ShapeDtypeStruct((m, dff), BF16),
            jax.ShapeDtypeStruct((m, dff), F32),
            jax.ShapeDtypeStruct((m, dff), F32),
        ],
        compiler_params=_cparams(("parallel",)),
        name="ffn_up_sample",
    )(xn, w_up, w_up, cw, cw, cb2, cb2, st, st)
    up = jnp.concatenate([ua, ub], axis=-1)
    new_state = jnp.concatenate([state[:, 1:, :], up[:, None, :]], axis=1)
    return act, new_state


def _ple_kernel(x_ref, xt_ref, p_ref, g_ref, wp_ref, wg_ref, o_ref, xn_ref, *, row_chunk):
    @pl.when(pl.program_id(1) == 0)
    def _():
        tm = x_ref.shape[0]
        for c in range(tm // row_chunk):
            rows = pl.ds(c * row_chunk, row_chunk)
            xn_ref[rows, :] = _rms_rows(x_ref[rows, :], g_ref[...]).astype(BF16)

    emb = _dot(p_ref[...].astype(BF16), wp_ref[...])
    gate = _dot(xn_ref[...], wg_ref[...])
    o_ref[...] = xt_ref[...] + emb * jax.nn.sigmoid(gate)


def _ple(x, p, g, w_ple, w_gate):
    m, d = x.shape
    kp = p.shape[1]
    tm = min(ROW_TILE, m)
    tn = COL_TILE
    row_chunk = min(256, tm)
    return pl.pallas_call(
        functools.partial(_ple_kernel, row_chunk=row_chunk),
        grid=(m // tm, d // tn),
        in_specs=[
            pl.BlockSpec((tm, d), lambda i, j: (i, 0)),
            pl.BlockSpec((tm, tn), lambda i, j: (i, j)),
            pl.BlockSpec((tm, kp), lambda i, j: (i, 0)),
            pl.BlockSpec((1, d), lambda i, j: (0, 0)),
            pl.BlockSpec((kp, tn), lambda i, j: (0, j)),
            pl.BlockSpec((d, tn), lambda i, j: (0, j)),
        ],
        out_specs=pl.BlockSpec((tm, tn), lambda i, j: (i, j)),
        out_shape=jax.ShapeDtypeStruct((m, d), F32),
        scratch_shapes=[pltpu.VMEM((tm, d), BF16)],
        compiler_params=_cparams(("parallel", "arbitrary")),
        name="ple",
    )(x, x, p, g.reshape(1, d), w_ple, w_gate)


def kernel(x_prompt, x_sample, cache_k, cache_v, state_ssm, state_conv, state_ffn_conv, page_table,
           p_prompt, p_sample, g_mix, w_in, g_q, g_k, sb_bias, conv_w, conv_b, dt_bias, a_log, d_skip, g_ssm,
           w_proj_a, w_proj_b, w_out, g_ffn, w_up, ffn_conv_w, ffn_conv_b, w_down, w_ple, g_ple, w_ple_gate):
    depth = w_in.shape[0]
    assert depth == 1
    batch, seq, d_model = x_prompt.shape
    db, dseq, _ = x_sample.shape
    assert dseq == 1
    _, n_pool, page, sb_heads, hd = cache_k.shape
    sb_width = sb_heads * hd
    _, _, heads, pdim, nstate = state_ssm.shape
    inner = heads * pdim
    conv_dim = state_conv.shape[-1]
    groups = (conv_dim - inner) // (2 * nstate)
    dims = (heads, pdim, nstate, groups)

    off_z = 3 * sb_width
    off_dt = off_z + inner + conv_dim
    off_ga = off_dt + heads
    w_in0 = w_in[0]
    w_tail = jnp.concatenate(
        [w_in0[:, off_dt:off_ga], jnp.zeros((d_model, COL_TILE - heads), F32), w_in0[:, off_ga:]], axis=1)
    tail_ga, tail_gb = COL_TILE, COL_TILE + d_model
    wup_b = w_up[0].astype(BF16)
    wa_b = w_proj_a[0].astype(BF16)
    wb_b = w_proj_b[0].astype(BF16)
    wout_b = w_out[0].astype(BF16)
    wdown_b = w_down[0].astype(BF16)
    wple_b = w_ple[0].astype(BF16)
    wgate_b = w_ple_gate[0].astype(BF16)
    gq = g_q[0].reshape(1, hd)
    gk = g_k[0].reshape(1, hd)

    def mixer_inputs(x2):
        xn = _rms_norm_bf16(x2, g_mix[0])
        proj = functools.partial(_proj, xn, w_in0, tn=PROJ_COL_TILE, hd=hd)
        (qn,) = proj(col0=0, ncols=sb_width, epilogue=functools.partial(_epi_q, hd=hd), extras=[gq],
                     out_kinds=["bf16"], name="proj_q")
        kn, knb = proj(col0=sb_width, ncols=sb_width, epilogue=functools.partial(_epi_k, hd=hd), extras=[gk],
                       out_kinds=["heads", "bf16"], name="proj_k")
        v, vbt = proj(col0=2 * sb_width, ncols=sb_width, epilogue=functools.partial(_epi_v, hd=hd), extras=[],
                      out_kinds=["heads", "bf16_t"], name="proj_v")
        (zx,) = proj(col0=off_z, ncols=inner + conv_dim, epilogue=_epi_plain, extras=[],
                     out_kinds=["f32"], name="proj_zx")
        (tail,) = _proj(xn, w_tail, col0=0, ncols=w_tail.shape[1], tn=COL_TILE, epilogue=_epi_plain, extras=[],
                        out_kinds=["f32"], hd=hd, name="proj_tail")
        return qn, kn, knb, v, vbt, zx, tail

    def merge_branches(x2, o_a, o_b, tail):
        merged = _merge(o_a, o_b, wa_b, wb_b, tail, off_ga=tail_ga, off_gb=tail_gb)
        return _resid_matmul_norm(x2, merged, wout_b, g_ffn[0])

    def ffn_down_and_embed(x1, act, p2):
        return _ple(_resid_matmul(x1, act, wdown_b), p2, g_ple[0], wple_b, wgate_b)

    xp2 = x_prompt.reshape(batch * seq, d_model)
    xs2 = x_sample.reshape(db, d_model)
    qn, kn, knb, v, vbt, zx, tail = mixer_inputs(xp2)
    qn_s, kn_s, _, v_s, _, zx_s, tail_s = mixer_inputs(xs2)
    n_pages = page_table.shape[1]

    o_a = _sb_prompt(qn, knb, vbt, sb_bias[0], batch=batch, seq=seq, heads=sb_heads, hd=hd)
    o_b, ssm_p, z_s = _ssd_prompt(zx, tail[:, :heads], conv_w[0], conv_b[0], dt_bias[0], a_log[0], d_skip[0],
                                  g_ssm[0], qn_s, cache_k, page_table, batch=batch, seq=seq, dims=dims)
    w_s = _sb_weights(z_s, sb_bias[0])
    cw = conv_w.shape[1]
    conv_p = zx.reshape(batch, seq, -1)[:, seq - (cw - 1):, inner:]
    x1, xn1 = merge_branches(xp2, o_a, o_b, tail)
    act, ffn_p, o_a_s = _ffn_up_prompt(xn1, wup_b, ffn_conv_w[0], ffn_conv_b[0], w_s, cache_v, page_table, seq=seq)
    yp = ffn_down_and_embed(x1, act, p_prompt[0].reshape(batch * seq, -1))
    k_p = kn.reshape(batch, seq, sb_heads, hd)
    v_p = v.reshape(batch, seq, sb_heads, hd)

    o_b_s, conv_s, ssm_s = _ssd_sample(
        zx_s[:, inner:], zx_s[:, :inner], tail_s[:, :heads], state_conv[0], state_ssm[0],
        conv_w[0], conv_b[0], dt_bias[0], a_log[0], d_skip[0], g_ssm[0], dims=dims)
    x1_s, xn1_s = merge_branches(xs2, o_a_s, o_b_s, tail_s)
    act_s, ffn_s = _ffn_up_sample(xn1_s, wup_b, ffn_conv_w[0], ffn_conv_b[0], state_ffn_conv[0])
    ys = ffn_down_and_embed(x1_s, act_s, p_sample[0].reshape(db, -1))

    return (yp.reshape(batch, seq, d_model), ys.reshape(db, 1, d_model),
            k_p[None], v_p[None], ssm_p[None], conv_p[None], ffn_p[None],
            kn_s.reshape(db, 1, sb_heads, hd)[None], v_s.reshape(db, 1, sb_heads, hd)[None],
            ssm_s[None], conv_s[None], ffn_s[None])
```

```python
import functools
import math

import jax
import jax.numpy as jnp
from jax import lax
from jax.experimental import pallas as pl
from jax.experimental.pallas import tpu as pltpu

F32 = jnp.float32
BF16 = jnp.bfloat16
EPS = 1e-6
LANES = 128
SUBLANES = 8
VMEM_LIMIT = 56 * 1024 * 1024
COL_TILE = 512
ROW_TILE = 1024
PROJ_COL_TILE = 1024
ATTN_BLOCK = 256
ATTN_HEADS_PER_STEP = 4
FFN_ROW_TILE = 512


def _cparams(sem):
    return pltpu.CompilerParams(dimension_semantics=sem, vmem_limit_bytes=VMEM_LIMIT)


def _dot(a, b):
    return jnp.dot(a, b, preferred_element_type=F32)


def _dot_nt(a, b):
    return lax.dot_general(a, b, (((1,), (1,)), ((), ())), preferred_element_type=F32)


def _split3(x):
    hi = x.astype(BF16)
    r = x - hi.astype(F32)
    mid = r.astype(BF16)
    lo = (r - mid.astype(F32)).astype(BF16)
    return hi, mid, lo


def _dot3(x, m):
    hi, mid, lo = _split3(x)
    return _dot(hi, m) + _dot(mid, m) + _dot(lo, m)


def _softplus(x):
    return jnp.maximum(x, 0.0) + jnp.log(1.0 + jnp.exp(-jnp.abs(x)))


LOG2E = math.log2(math.e)


def _softplus2(x):
    neg_abs = lax.bitcast_convert_type(lax.bitcast_convert_type(x, jnp.uint32) | jnp.uint32(0x80000000), F32)
    return jnp.maximum(x, 0.0) + jnp.log2(1.0 + jnp.exp2(neg_abs))


def _silu(x):
    return x * jax.nn.sigmoid(x)


def _rms_rows(x, g):
    ms = jnp.mean(x * x, axis=-1, keepdims=True)
    return x * lax.rsqrt(ms + EPS) * g


def _rms_norm_kernel(x_ref, g_ref, o_ref):
    o_ref[...] = _rms_rows(x_ref[...], g_ref[...]).astype(BF16)


def _rms_norm_bf16(x, g):
    m, k = x.shape
    tm = min(256, m)
    return pl.pallas_call(
        _rms_norm_kernel,
        grid=(m // tm,),
        in_specs=[pl.BlockSpec((tm, k), lambda i: (i, 0)), pl.BlockSpec((1, k), lambda i: (0, 0))],
        out_specs=pl.BlockSpec((tm, k), lambda i: (i, 0)),
        out_shape=jax.ShapeDtypeStruct((m, k), BF16),
        compiler_params=_cparams(("parallel",)),
        name="rms_norm",
    )(x, g.reshape(1, k))


def _proj_kernel(x_ref, *rest, n_w, n_extra, n_out, epilogue, first_tile_rows):
    w_refs = rest[:n_w]
    extras = rest[n_w:n_w + n_extra]
    outs = rest[n_w + n_extra:n_w + n_extra + n_out]
    wb_ref = rest[n_w + n_extra + n_out]

    @pl.when(pl.program_id(1) == 0)
    def _():
        w = jnp.concatenate([r[...] for r in w_refs], axis=0) if n_w > 1 else w_refs[0][...]
        if first_tile_rows is not None:
            row = lax.broadcasted_iota(jnp.int32, w.shape, 0)
            w = jnp.where((pl.program_id(0) > 0) | (row < first_tile_rows), w, 0.0)
        wb_ref[...] = w.astype(BF16)

    epilogue(_dot_nt(x_ref[...], wb_ref[...]), *extras, *outs)


def _epi_plain(acc, o_ref):
    o_ref[...] = acc


def _epi_q(acc, g_ref, qn_ref, *, hd):
    scale = LOG2E / math.sqrt(hd)
    for h in range(acc.shape[1] // hd):
        lanes = slice(h * hd, (h + 1) * hd)
        qn_ref[:, lanes] = (_rms_rows(acc[:, lanes], g_ref[...]) * scale).astype(BF16)


def _epi_k(acc, g_ref, kn_ref, knb_ref, *, hd):
    for h in range(acc.shape[1] // hd):
        lanes = slice(h * hd, (h + 1) * hd)
        kn = _rms_rows(acc[:, lanes], g_ref[...])
        kn_ref[:, h, :] = kn
        knb_ref[:, lanes] = kn.astype(BF16)


def _epi_v(acc, v_ref, vbt_ref, *, hd):
    for h in range(acc.shape[1] // hd):
        lanes = slice(h * hd, (h + 1) * hd)
        v_ref[:, h, :] = acc[:, lanes]
        vbt_ref[lanes, :] = acc[:, lanes].T.astype(BF16)


def _proj(xn, wt, *, ncols, tn, w_rows, w_block, epilogue, extras, out_kinds, hd, name, first_tile_rows=None):
    m, k = xn.shape
    tm = min(ROW_TILE, m)
    assert ncols % tn == 0 and tn % w_rows == 0
    n_w = tn // w_rows
    out_specs, out_shapes = [], []
    for kind in out_kinds:
        if kind == "heads":
            out_specs.append(pl.BlockSpec((tm, tn // hd, hd), lambda j, i: (i, j, 0)))
            out_shapes.append(jax.ShapeDtypeStruct((m, ncols // hd, hd), F32))
        elif kind == "bf16_t":
            out_specs.append(pl.BlockSpec((tn, tm), lambda j, i: (j, i)))
            out_shapes.append(jax.ShapeDtypeStruct((ncols, m), BF16))
        else:
            out_specs.append(pl.BlockSpec((tm, tn), lambda j, i: (i, j)))
            out_shapes.append(jax.ShapeDtypeStruct((m, ncols), F32 if kind == "f32" else BF16))
    extra_specs = [pl.BlockSpec(e.shape, lambda j, i: (0, 0)) for e in extras]

    def w_spec(r):
        return pl.BlockSpec((w_rows, k), lambda j, i: (w_block(j, r), 0))

    return pl.pallas_call(
        functools.partial(_proj_kernel, n_w=n_w, n_extra=len(extras), n_out=len(out_kinds), epilogue=epilogue,
                          first_tile_rows=first_tile_rows),
        grid=(ncols // tn, m // tm),
        in_specs=[pl.BlockSpec((tm, k), lambda j, i: (i, 0))] + [w_spec(r) for r in range(n_w)] + extra_specs,
        out_specs=out_specs,
        out_shape=out_shapes,
        scratch_shapes=[pltpu.VMEM((tn, k), BF16)],
        compiler_params=_cparams(("parallel", "arbitrary")),
        name=name,
    )(xn, *([wt] * n_w), *extras)


def _sb_prompt_kernel(bias_ref, q_ref, k_ref, vt_ref, tri_ref, o_ref, *, tq, tk, hd, hps):
    hb = pl.program_id(1)
    qi = pl.program_id(2)
    tri = tri_ref[...]

    def block(k0, state, masked):
        if masked:
            row = lax.broadcasted_iota(jnp.int32, (tq, tk), 0)
            col = lax.broadcasted_iota(jnp.int32, (tq, tk), 1)
            keep = col < row
        heads = range(hps)
        lanes = [slice(g * hd, (g + 1) * hd) for g in heads]
        zs = [_dot_nt(q_ref[:, lanes[g]], k_ref[pl.ds(k0, tk), lanes[g]]) + bias_ref[hb * hps + g] for g in heads]
        sps = [_softplus2(z) for z in zs]
        if masked:
            sps = [jnp.where(keep, sp, 0.0) for sp in sps]
        spb = [sp.astype(BF16) for sp in sps]
        tails = [_dot(sb, tri) for sb in spb]
        ws = [jnp.exp2(zs[g] - sps[g] - tails[g] - state[g][0]) for g in heads]
        if masked:
            ws = [jnp.where(keep, w, 0.0) for w in ws]
        pvs = [_dot_nt(vt_ref[lanes[g], pl.ds(k0, tk)], ws[g].astype(BF16)) for g in heads]
        sums = [tails[g][:, 0:1] + spb[g][:, 0:1].astype(F32) for g in heads]
        return tuple((state[g][0] + sums[g], state[g][1] + pvs[g]) for g in heads)

    state = tuple((jnp.zeros((tq, 1), F32), jnp.zeros((hd, tq), F32)) for _ in range(hps))
    state = block(pl.multiple_of(qi * tq, tq), state, True)

    def body(it, st):
        return block(pl.multiple_of((qi - 1 - it) * tk, tk), st, False)

    state = lax.fori_loop(0, qi, body, state)
    for g in range(hps):
        o_ref[:, g * hd:(g + 1) * hd] = state[g][1].T.astype(o_ref.dtype)


def _sb_prompt(qn, knb, vbt, sb_bias, *, batch, seq, heads, hd):
    tq = tk = min(ATTN_BLOCK, seq)
    hps = ATTN_HEADS_PER_STEP
    assert seq % tq == 0 and heads % hps == 0
    width = heads * hd
    q3 = qn.reshape(batch, seq, width)
    k3 = knb.reshape(batch, seq, width)
    tri = (jnp.arange(tk)[:, None] > jnp.arange(tk)[None, :]).astype(BF16)
    out = pl.pallas_call(
        functools.partial(_sb_prompt_kernel, tq=tq, tk=tk, hd=hd, hps=hps),
        grid=(batch, heads // hps, seq // tq),
        in_specs=[
            pl.BlockSpec(memory_space=pltpu.SMEM),
            pl.BlockSpec((None, tq, hps * hd), lambda b, h, i: (b, i, h)),
            pl.BlockSpec((None, seq, hps * hd), lambda b, h, i: (b, 0, h)),
            pl.BlockSpec((hps * hd, seq), lambda b, h, i: (h, b)),
            pl.BlockSpec((tk, tk), lambda b, h, i: (0, 0)),
        ],
        out_specs=pl.BlockSpec((None, tq, hps * hd), lambda b, h, i: (b, i, h)),
        out_shape=jax.ShapeDtypeStruct((batch, seq, width), BF16),
        compiler_params=_cparams(("parallel", "parallel", "arbitrary")),
        name="sb_attn_prompt",
    )(sb_bias.astype(F32) * LOG2E, q3, k3, vbt, tri)
    return out.reshape(batch * seq, width)


class _PageJob:
    def __init__(self, cache, page_table, n_steps):
        self.db, self.n_pages = page_table.shape
        _, _, self.page, self.heads, self.hd = cache.shape
        self.total = self.db * self.n_pages
        self.n_steps = n_steps
        self.pps = -(-self.total // n_steps)
        self.pps_pad = -(-self.pps // SUBLANES) * SUBLANES
        self.width = self.page * self.heads

    def page_specs(self, step_of):
        def spec(p):
            def index(*args):
                pt = args[-1]
                flat = jnp.minimum(step_of(*args[:-1]) * self.pps + p, self.total - 1)
                return (0, pt[flat], 0, 0, 0)
            return pl.BlockSpec((None, None, self.page, self.heads, self.hd), index)
        return [spec(p) for p in range(self.pps)]

    def sequence_of(self, step, p):
        return jnp.minimum(step * self.pps + p, self.total - 1) // self.n_pages

    def rows_to_steps(self, rows):
        rows = jnp.pad(rows, ((0, self.n_steps * self.pps - self.total), (0, 0)))
        rows = rows.reshape(self.n_steps, self.pps, self.width)
        return jnp.pad(rows, ((0, 0), (0, self.pps_pad - self.pps), (0, 0)))

    def steps_to_rows(self, blocks):
        return blocks[:, :self.pps].reshape(self.n_steps * self.pps, self.width)[:self.total]


def _own_head(heads, width):
    row = lax.broadcasted_iota(jnp.int32, (heads, width), 0)
    col = lax.broadcasted_iota(jnp.int32, (heads, width), 1)
    return col % heads == row


def _flat_page(page_ref, heads):
    kp = page_ref[...]
    return kp.reshape(kp.shape[0] * heads, kp.shape[2]).astype(BF16)


def _score_page(job, step, p, q_ref, k_refs):
    q = q_ref[job.sequence_of(step, p)]
    full = _dot_nt(q, _flat_page(k_refs[p], job.heads))
    return jnp.sum(jnp.where(_own_head(job.heads, job.width), full, 0.0), axis=0, keepdims=True)


def _store_scores(job, rows, z_ref):
    if job.pps_pad > job.pps:
        rows = rows + [jnp.zeros((job.pps_pad - job.pps, job.width), F32)]
    z_ref[...] = jnp.concatenate(rows, axis=0)


def _sum_weighted_pages(job, step, w_ref, v_refs, acc_ref):
    own = _own_head(job.heads, job.width)
    w_all = w_ref[...]
    for p in range(job.pps):
        wm = jnp.where(own, jnp.broadcast_to(w_all[p:p + 1, :], (job.heads, job.width)), 0.0).astype(BF16)
        b = job.sequence_of(step, p)
        acc_ref[b] = acc_ref[b] + _dot(wm, _flat_page(v_refs[p], job.heads))


def _sb_weights_kernel(z_ref, bias_ref, w_ref, *, heads):
    n_pages, width = z_ref.shape
    lane = lax.broadcasted_iota(jnp.int32, (n_pages, width), 1)
    row = lax.broadcasted_iota(jnp.int32, (n_pages, width), 0)
    z = z_ref[...] + bias_ref[...]
    sp = _softplus2(z)

    incl = sp
    shift = heads
    while shift < width:
        moved = pltpu.roll(incl, width - shift, 1)
        incl = incl + jnp.where(lane < width - shift, moved, 0.0)
        shift *= 2
    total = jnp.where(lane < heads, incl, 0.0)
    shift = heads
    while shift < width:
        total = total + pltpu.roll(total, shift, 1)
        shift *= 2
    below = total
    shift = 1
    while shift < n_pages:
        moved = pltpu.roll(below, n_pages - shift, 0)
        below = below + jnp.where(row < n_pages - shift, moved, 0.0)
        shift *= 2
    later = (incl - sp) + (below - total)
    w_ref[...] = jnp.exp2(z - sp - later)


def _sb_weights(z, sb_bias):
    db, n_pages, width = z.shape
    heads = sb_bias.shape[0]
    assert width % heads == 0 and (width // heads) & (width // heads - 1) == 0 and n_pages & (n_pages - 1) == 0
    bias = jnp.tile(sb_bias.astype(F32) * LOG2E, width // heads).reshape(1, width)
    return pl.pallas_call(
        functools.partial(_sb_weights_kernel, heads=heads),
        grid=(db,),
        in_specs=[
            pl.BlockSpec((None, n_pages, width), lambda b: (b, 0, 0)),
            pl.BlockSpec((1, width), lambda b: (0, 0)),
        ],
        out_specs=pl.BlockSpec((None, n_pages, width), lambda b: (b, 0, 0)),
        out_shape=jax.ShapeDtypeStruct((db, n_pages, width), F32),
        compiler_params=_cparams(("parallel",)),
        name="sb_sample_weights",
    )(z, bias)


def _ssd_prompt_kernel(pt_ref, x_ref, bm_ref, cm_ref, z_ref, dt_ref,
                       wx_ref, wb_ref, wc_ref, bx_ref, bb_ref, bc_ref,
                       dtb_ref, al_ref, dskip_ref, gssm_ref, triu_ref, q_ref, *rest,
                       chunk, hpg, pdim, nstate, conv_w, job):
    k_refs = rest[:job.pps]
    o_ref, st_ref, sc_ref, ext_ref, ht_ref = rest[job.pps:]
    c = pl.program_id(2)
    gw = hpg * pdim
    halo = SUBLANES
    widths = (gw, nstate, nstate)
    offs = (0, gw, gw + nstate)

    @pl.when(c == 0)
    def _():
        ext_ref[0:halo, :] = jnp.zeros((halo, ext_ref.shape[1]), F32)
        ht_ref[...] = jnp.zeros_like(ht_ref)

    raws = (x_ref, bm_ref, cm_ref)
    wrefs = (wx_ref, wb_ref, wc_ref)
    brefs = (bx_ref, bb_ref, bc_ref)
    conv = []
    for raw, wref, bref, off, wd in zip(raws, wrefs, brefs, offs, widths):
        ext_ref[halo:halo + chunk, off:off + wd] = raw[...]
        acc = jnp.broadcast_to(bref[...], (chunk, wd))
        for j in range(conv_w):
            start = halo - (conv_w - 1) + j
            acc = acc + ext_ref[start:start + chunk, off:off + wd] * wref[j:j + 1, :]
        conv.append(_silu(acc))
    ext_ref[0:halo, :] = ext_ref[chunk:chunk + halo, :]
    xs, bmat, cmat = conv

    dt_r = _softplus(dt_ref[...] + dtb_ref[...])
    a_r = -jnp.exp(al_ref[...])
    da_r = _dot3(dt_r * a_r, triu_ref[...])
    dt_c = dt_r.T
    da_c = da_r.T

    bmat_b = bmat.astype(BF16)
    cmat_b = cmat.astype(BF16)
    row = lax.broadcasted_iota(jnp.int32, (chunk, chunk), 0)
    col = lax.broadcasted_iota(jnp.int32, (chunk, chunk), 1)
    causal = col <= row
    cb = jnp.where(causal, _dot_nt(cmat_b, bmat_b), 0.0)
    bmat_t = bmat.T.astype(BF16)
    first = lax.broadcasted_iota(jnp.int32, (chunk, LANES), 1) < pdim

    heads_per_tile = LANES // pdim
    assert heads_per_tile == 2
    n_tiles = hpg // heads_per_tile
    step = (pl.program_id(0) * pl.num_programs(1) + pl.program_id(1)) * pl.num_programs(2) + c
    scores = []
    us = []
    ssq = jnp.zeros((chunk, 1), F32)
    for t in range(n_tiles):
        for p in range(t * job.pps // n_tiles, (t + 1) * job.pps // n_tiles):
            scores.append(_score_page(job, step, p, q_ref, k_refs))
        r0 = t * heads_per_tile
        lanes = slice(t * LANES, (t + 1) * LANES)
        da_bc = [jnp.broadcast_to(da_c[:, r:r + 1], (chunk, LANES)) for r in (r0, r0 + 1)]
        dt_bc = [jnp.broadcast_to(dt_c[:, r:r + 1], (chunk, LANES)) for r in (r0, r0 + 1)]
        da_pair = jnp.where(first, da_bc[0], da_bc[1])
        dt_pair = jnp.where(first, dt_bc[0], dt_bc[1])
        last_pair = da_pair[chunk - 1:chunk, :]
        decay = []
        for i, r in enumerate((r0, r0 + 1)):
            diff = da_bc[i] - da_r[r:r + 1, :]
            decay.append((cb * jnp.exp(jnp.minimum(diff, 0.0))).astype(BF16))
        m_cat = jnp.concatenate(decay, axis=1)
        xp = xs[:, lanes]
        xdt = xp * dt_pair
        xbd = jnp.concatenate([jnp.where(first, xdt, 0.0), jnp.where(first, 0.0, xdt)], axis=0).astype(BF16)
        y = _dot(m_cat, xbd)
        ht_old = ht_ref[:, lanes]
        y = y + _dot(cmat_b, ht_old.astype(BF16)) * jnp.exp(da_pair)
        xw = (xdt * jnp.exp(last_pair - da_pair)).astype(BF16)
        ht_ref[:, lanes] = ht_old * jnp.exp(last_pair) + _dot(bmat_t, xw)
        y = y + xp * dskip_ref[:, lanes]
        u = y * _silu(z_ref[:, lanes])
        ssq = ssq + jnp.sum(u * u, axis=1, keepdims=True)
        us.append(u)
    inv = lax.rsqrt(ssq / gw + EPS)
    for t, u in enumerate(us):
        lanes = slice(t * LANES, (t + 1) * LANES)
        o_ref[:, lanes] = (u * inv * gssm_ref[:, lanes]).astype(o_ref.dtype)
    _store_scores(job, scores, sc_ref)

    @pl.when(c == pl.num_programs(2) - 1)
    def _():
        st_ref[...] = ht_ref[...]


def _ssd_prompt(zx, dt_raw, conv_w, conv_b, dt_bias, a_log, d_skip, g_ssm, qn_s, cache_k, page_table, *,
                batch, seq, dims):
    heads, pdim, nstate, groups = dims
    hpg = heads // groups
    gw = hpg * pdim
    inner = heads * pdim
    chunk = min(LANES, seq)
    n_chunks = seq // chunk
    cw = conv_w.shape[0]
    zx3 = zx.reshape(batch, seq, zx.shape[1])
    dtt = dt_raw.T
    idx = jnp.arange(chunk)
    triu = (idx[:, None] <= idx[None, :]).astype(BF16)
    job = _PageJob(cache_k, page_table, batch * groups * n_chunks)

    zcol0 = 0
    xcol0 = inner // gw
    bcol0 = (2 * inner) // nstate
    ccol0 = (2 * inner + groups * nstate) // nstate
    wb0 = inner // nstate
    wc0 = (inner + groups * nstate) // nstate

    def step_of(b, g, c):
        return (b * groups + g) * n_chunks + c

    def tok(width, col0):
        return pl.BlockSpec((None, chunk, width), lambda b, g, c, pt: (b, c, col0 + g))

    def par(rows, width, col0):
        return pl.BlockSpec((rows, width), lambda b, g, c, pt: (0, col0 + g))

    def grp(shape):
        return pl.BlockSpec((None,) + shape, lambda b, g, c, pt: (g, 0, 0))

    conv_b2 = conv_b.reshape(1, -1)
    o_b, st, scores = pl.pallas_call(
        functools.partial(_ssd_prompt_kernel, chunk=chunk, hpg=hpg, pdim=pdim, nstate=nstate, conv_w=cw, job=job),
        grid_spec=pltpu.PrefetchScalarGridSpec(
            num_scalar_prefetch=1,
            grid=(batch, groups, n_chunks),
            in_specs=[
                tok(gw, xcol0), tok(nstate, bcol0), tok(nstate, ccol0), tok(gw, zcol0),
                pl.BlockSpec((hpg, chunk), lambda b, g, c, pt: (g, b * n_chunks + c)),
                par(cw, gw, 0), par(cw, nstate, wb0), par(cw, nstate, wc0),
                par(1, gw, 0), par(1, nstate, wb0), par(1, nstate, wc0),
                grp((hpg, 1)), grp((hpg, 1)),
                par(1, gw, 0), par(1, gw, 0),
                pl.BlockSpec((chunk, chunk), lambda b, g, c, pt: (0, 0)),
                pl.BlockSpec((job.db, job.heads, job.hd), lambda b, g, c, pt: (0, 0, 0)),
            ] + job.page_specs(step_of),
            out_specs=[
                pl.BlockSpec((None, chunk, gw), lambda b, g, c, pt: (b, c, g)),
                pl.BlockSpec((None, None, nstate, gw), lambda b, g, c, pt: (b, g, 0, 0)),
                pl.BlockSpec((None, job.pps_pad, job.width), lambda b, g, c, pt: (step_of(b, g, c), 0, 0)),
            ],
            scratch_shapes=[
                pltpu.VMEM((chunk + SUBLANES, gw + 2 * nstate), F32),
                pltpu.VMEM((nstate, gw), F32),
            ],
        ),
        out_shape=[
            jax.ShapeDtypeStruct((batch, seq, inner), BF16),
            jax.ShapeDtypeStruct((batch, groups, nstate, gw), F32),
            jax.ShapeDtypeStruct((job.n_steps, job.pps_pad, job.width), F32),
        ],
        compiler_params=_cparams(("parallel", "parallel", "arbitrary")),
        name="ssd_prompt",
    )(page_table.reshape(-1), zx3, zx3, zx3, zx3, dtt,
      conv_w, conv_w, conv_w, conv_b2, conv_b2, conv_b2,
      dt_bias.reshape(groups, hpg, 1), a_log.reshape(groups, hpg, 1),
      jnp.repeat(d_skip, pdim).reshape(1, inner), g_ssm.reshape(1, inner),
      triu, qn_s.reshape(job.db, job.heads, job.hd), *([cache_k] * job.pps))
    st = st.reshape(batch, groups, nstate, hpg, pdim).transpose(0, 1, 3, 4, 2)
    scores = job.steps_to_rows(scores).reshape(job.db, job.n_pages, job.width)
    return o_b.reshape(batch * seq, inner), st.reshape(batch, heads, pdim, nstate), scores


def _ssd_sample_kernel(xbc_ref, z_ref, dt_ref, cst_ref, h0_ref,
                       cw_ref, cb_ref, dtb_ref, al_ref, dskip_ref, gssm_ref, expand_ref,
                       o_ref, cnew_ref, h_ref, *, inner, nstate, groups, conv_w):
    gw = inner // groups
    x = xbc_ref[...]
    acc = cb_ref[...] + x * cw_ref[conv_w - 1:conv_w, :]
    for j in range(conv_w - 1):
        acc = acc + cst_ref[j:j + 1, :] * cw_ref[j:j + 1, :]
        if j > 0:
            cnew_ref[j - 1:j, :] = cst_ref[j:j + 1, :]
    cnew_ref[conv_w - 2:conv_w - 1, :] = x
    xc = _silu(acc)
    xs = xc[:, :inner]
    bv = xc[:, inner:inner + groups * nstate]
    cv = xc[:, inner + groups * nstate:]

    dt8 = jnp.broadcast_to(dt_ref[...], (SUBLANES, dt_ref.shape[1]))
    dt = _softplus(_dot3(dt8, expand_ref[...])[0:1, :] + dtb_ref[...])
    dec = jnp.exp(dt * (-jnp.exp(al_ref[...])))
    xdt = xs * dt
    row8 = lax.broadcasted_iota(jnp.int32, (SUBLANES, gw), 0)
    ys = []
    for g in range(groups):
        lanes = slice(g * gw, (g + 1) * gw)
        glanes = slice(g * nstate, (g + 1) * nstate)
        stacked = jnp.where(row8 == 0, xdt[:, lanes], jnp.where(row8 == 1, dec[:, lanes], 0.0))
        cols = stacked.T
        hn = h0_ref[lanes, :] * cols[:, 1:2] + cols[:, 0:1] * bv[:, glanes]
        h_ref[lanes, :] = hn
        cg = jnp.broadcast_to(cv[:, glanes], (SUBLANES, nstate)).astype(BF16)
        ys.append(_dot_nt(cg, hn.astype(BF16))[0:1, :])
    y = jnp.concatenate(ys, axis=1) + xs * dskip_ref[...]
    u = y * _silu(z_ref[...])
    for g in range(groups):
        lanes = slice(g * gw, (g + 1) * gw)
        ug = u[:, lanes]
        inv = lax.rsqrt(jnp.mean(ug * ug, axis=1, keepdims=True) + EPS)
        o_ref[:, lanes] = (ug * inv * gssm_ref[:, lanes]).astype(o_ref.dtype)


def _ssd_sample(xbc, z, dt_raw, state_conv, state_ssm, conv_w, conv_b, dt_bias, a_log, d_skip, g_ssm, *, dims):
    heads, pdim, nstate, groups = dims
    inner = heads * pdim
    db, conv_dim = xbc.shape
    cw = conv_w.shape[0]
    rep = lambda v: jnp.repeat(v.astype(F32), pdim).reshape(1, inner)
    expand = (jnp.arange(heads)[:, None] == (jnp.arange(inner)[None, :] // pdim)).astype(BF16)
    row = lambda width: pl.BlockSpec((None, 1, width), lambda b: (b, 0, 0))
    full = lambda shape: pl.BlockSpec(shape, lambda b: (0,) * len(shape))
    o_b, conv_new, h_new = pl.pallas_call(
        functools.partial(_ssd_sample_kernel, inner=inner, nstate=nstate, groups=groups, conv_w=cw),
        grid=(db,),
        in_specs=[
            row(conv_dim), row(inner), row(heads),
            pl.BlockSpec((None, cw - 1, conv_dim), lambda b: (b, 0, 0)),
            pl.BlockSpec((None, inner, nstate), lambda b: (b, 0, 0)),
            full((cw, conv_dim)), full((1, conv_dim)),
            full((1, inner)), full((1, inner)), full((1, inner)), full((1, inner)),
            full((heads, inner)),
        ],
        out_specs=[
            row(inner),
            pl.BlockSpec((None, cw - 1, conv_dim), lambda b: (b, 0, 0)),
            pl.BlockSpec((None, inner, nstate), lambda b: (b, 0, 0)),
        ],
        out_shape=[
            jax.ShapeDtypeStruct((db, 1, inner), BF16),
            jax.ShapeDtypeStruct((db, cw - 1, conv_dim), F32),
            jax.ShapeDtypeStruct((db, inner, nstate), F32),
        ],
        compiler_params=_cparams(("parallel",)),
        name="ssd_sample",
    )(xbc.reshape(db, 1, conv_dim), z.reshape(db, 1, inner), dt_raw.reshape(db, 1, heads),
      state_conv, state_ssm.reshape(db, inner, nstate),
      conv_w, conv_b.reshape(1, conv_dim), rep(dt_bias), rep(a_log), rep(d_skip), g_ssm.reshape(1, inner),
      expand)
    return o_b.reshape(db, inner), conv_new, h_new.reshape(db, heads, pdim, nstate)


def _merge_kernel(oa_ref, ob_ref, wa_ref, wb_ref, ga_ref, gb_ref, o_ref):
    a = _dot(oa_ref[...], wa_ref[...])
    b = _dot(ob_ref[...], wb_ref[...])
    o_ref[...] = (jax.nn.sigmoid(ga_ref[...]) * a + jax.nn.sigmoid(gb_ref[...]) * b).astype(o_ref.dtype)


def _merge(o_a, o_b, wa, wb, gates, *, off_ga, off_gb):
    m, ka = o_a.shape
    kb = o_b.shape[1]
    n = wa.shape[1]
    tm = min(ROW_TILE, m)
    tn = COL_TILE
    ga0, gb0 = off_ga // tn, off_gb // tn
    return pl.pallas_call(
        _merge_kernel,
        grid=(m // tm, n // tn),
        in_specs=[
            pl.BlockSpec((tm, ka), lambda i, j: (i, 0)),
            pl.BlockSpec((tm, kb), lambda i, j: (i, 0)),
            pl.BlockSpec((ka, tn), lambda i, j: (0, j)),
            pl.BlockSpec((kb, tn), lambda i, j: (0, j)),
            pl.BlockSpec((tm, tn), lambda i, j: (i, ga0 + j)),
            pl.BlockSpec((tm, tn), lambda i, j: (i, gb0 + j)),
        ],
        out_specs=pl.BlockSpec((tm, tn), lambda i, j: (i, j)),
        out_shape=jax.ShapeDtypeStruct((m, n), BF16),
        compiler_params=_cparams(("parallel", "arbitrary")),
        name="merge",
    )(o_a, o_b, wa, wb, gates, gates)


def _resid_mm_kernel(x_ref, a_ref, w_ref, g_ref, o_ref, on_ref):
    y = x_ref[...] + _dot(a_ref[...], w_ref[...])
    o_ref[...] = y
    on_ref[...] = _rms_rows(y, g_ref[...]).astype(BF16)


def _resid_matmul_norm(x, a, w, g):
    m, n = x.shape
    k = a.shape[1]
    tm = min(256, m)
    return pl.pallas_call(
        _resid_mm_kernel,
        grid=(m // tm,),
        in_specs=[
            pl.BlockSpec((tm, n), lambda i: (i, 0)),
            pl.BlockSpec((tm, k), lambda i: (i, 0)),
            pl.BlockSpec((k, n), lambda i: (0, 0)),
            pl.BlockSpec((1, n), lambda i: (0, 0)),
        ],
        out_specs=[pl.BlockSpec((tm, n), lambda i: (i, 0)), pl.BlockSpec((tm, n), lambda i: (i, 0))],
        out_shape=[jax.ShapeDtypeStruct((m, n), F32), jax.ShapeDtypeStruct((m, n), BF16)],
        compiler_params=_cparams(("parallel",)),
        name="resid_matmul_norm",
    )(x, a, w, g.reshape(1, n))


def _resid_mm_tiled_kernel(x_ref, a_ref, w_ref, o_ref):
    o_ref[...] = x_ref[...] + _dot(a_ref[...], w_ref[...])


def _resid_matmul(x, a, w):
    m, n = x.shape
    k = a.shape[1]
    tm = min(ROW_TILE, m)
    tn = COL_TILE
    return pl.pallas_call(
        _resid_mm_tiled_kernel,
        grid=(m // tm, n // tn),
        in_specs=[
            pl.BlockSpec((tm, tn), lambda i, j: (i, j)),
            pl.BlockSpec((tm, k), lambda i, j: (i, 0)),
            pl.BlockSpec((k, tn), lambda i, j: (0, j)),
        ],
        out_specs=pl.BlockSpec((tm, tn), lambda i, j: (i, j)),
        out_shape=jax.ShapeDtypeStruct((m, n), F32),
        compiler_params=_cparams(("parallel", "arbitrary")),
        name="resid_matmul",
    )(x, a, w)


def _ffn_up_prompt_kernel(pt_ref, xn_ref, wa_ref, wb_ref, cwa_ref, cwb_ref, cba_ref, cbb_ref, pw_ref, *rest,
                          seq, conv_w, job):
    v_refs = rest[:job.pps]
    act_ref, ta_ref, tb_ref, oa_ref, exta_ref, extb_ref, acc_ref = rest[job.pps:]
    i = pl.program_id(1)
    step = pl.program_id(0) * pl.num_programs(1) + i
    tm = xn_ref.shape[0]
    halo = SUBLANES

    @pl.when(step == 0)
    def _():
        acc_ref[...] = jnp.zeros_like(acc_ref)

    xn = xn_ref[...]
    start_of_seq = (i * tm) % seq == 0
    outs = []
    for w_ref, cw_ref, cb_ref, ext_ref, tail_ref in ((wa_ref, cwa_ref, cba_ref, exta_ref, ta_ref),
                                                     (wb_ref, cwb_ref, cbb_ref, extb_ref, tb_ref)):
        up = _dot(xn, w_ref[...])
        history = ext_ref[0:halo, :]
        ext_ref[0:halo, :] = jnp.where(start_of_seq, jnp.zeros_like(history), history)
        ext_ref[halo:halo + tm, :] = up
        acc = jnp.broadcast_to(cb_ref[...], up.shape)
        for j in range(conv_w):
            start = halo - (conv_w - 1) + j
            acc = acc + ext_ref[start:start + tm, :] * cw_ref[j:j + 1, :]
        tail = ext_ref[tm:tm + halo, :]
        ext_ref[0:halo, :] = tail
        tail_ref[...] = tail
        outs.append(acc)
    act_ref[...] = (_silu(outs[0]) * outs[1]).astype(act_ref.dtype)

    _sum_weighted_pages(job, step, pw_ref, v_refs, acc_ref)

    @pl.when(step == job.n_steps - 1)
    def _():
        oa_ref[...] = acc_ref[...].astype(oa_ref.dtype)


def _ffn_up_prompt(xn, w_up, cw, cb, w_s, cache_v, page_table, *, seq):
    m, k = xn.shape
    n2 = w_up.shape[1]
    dff = n2 // 2
    tm = min(FFN_ROW_TILE, m, seq)
    tn = COL_TILE
    nj = dff // tn
    ni = m // tm
    cwid = cw.shape[0]
    cb2 = cb.reshape(1, n2)
    job = _PageJob(cache_v, page_table, nj * ni)
    w_steps = job.rows_to_steps(w_s.reshape(job.total, job.width))

    def step_of(j, i):
        return j * ni + i

    act, ta, tb, o_a = pl.pallas_call(
        functools.partial(_ffn_up_prompt_kernel, seq=seq, conv_w=cwid, job=job),
        grid_spec=pltpu.PrefetchScalarGridSpec(
            num_scalar_prefetch=1,
            grid=(nj, ni),
            in_specs=[
                pl.BlockSpec((tm, k), lambda j, i, pt: (i, 0)),
                pl.BlockSpec((k, tn), lambda j, i, pt: (0, j)),
                pl.BlockSpec((k, tn), lambda j, i, pt: (0, nj + j)),
                pl.BlockSpec((cwid, tn), lambda j, i, pt: (0, j)),
                pl.BlockSpec((cwid, tn), lambda j, i, pt: (0, nj + j)),
                pl.BlockSpec((1, tn), lambda j, i, pt: (0, j)),
                pl.BlockSpec((1, tn), lambda j, i, pt: (0, nj + j)),
                pl.BlockSpec((None, job.pps_pad, job.width), lambda j, i, pt: (step_of(j, i), 0, 0)),
            ] + job.page_specs(step_of),
            out_specs=[
                pl.BlockSpec((tm, tn), lambda j, i, pt: (i, j)),
                pl.BlockSpec((None, SUBLANES, tn), lambda j, i, pt: (i, 0, j)),
                pl.BlockSpec((None, SUBLANES, tn), lambda j, i, pt: (i, 0, j)),
                pl.BlockSpec((job.db, job.heads, job.hd), lambda j, i, pt: (0, 0, 0)),
            ],
            scratch_shapes=[pltpu.VMEM((tm + SUBLANES, tn), F32), pltpu.VMEM((tm + SUBLANES, tn), F32),
                            pltpu.VMEM((job.db, job.heads, job.hd), F32)],
        ),
        out_shape=[
            jax.ShapeDtypeStruct((m, dff), BF16),
            jax.ShapeDtypeStruct((ni, SUBLANES, dff), F32),
            jax.ShapeDtypeStruct((ni, SUBLANES, dff), F32),
            jax.ShapeDtypeStruct((job.db, job.heads, job.hd), BF16),
        ],
        compiler_params=_cparams(("arbitrary", "arbitrary")),
        name="ffn_up_prompt",
    )(page_table.reshape(-1), xn, w_up, w_up, cw, cw, cb2, cb2, w_steps, *([cache_v] * job.pps))
    per_seq = seq // tm
    tails = jnp.concatenate([ta, tb], axis=-1)[per_seq - 1::per_seq, SUBLANES - (cwid - 1):, :]
    return act, tails, o_a.reshape(job.db, job.heads * job.hd)


def _ffn_up_sample_kernel(xn_ref, wa_ref, wb_ref, cwa_ref, cwb_ref, cba_ref, cbb_ref, sta_ref, stb_ref,
                          act_ref, ua_ref, ub_ref, *, conv_w):
    xn = xn_ref[...]
    outs = []
    for w_ref, cw_ref, cb_ref, st_ref, u_ref in ((wa_ref, cwa_ref, cba_ref, sta_ref, ua_ref),
                                                 (wb_ref, cwb_ref, cbb_ref, stb_ref, ub_ref)):
        up = _dot(xn, w_ref[...])
        u_ref[...] = up
        acc = cb_ref[...] + up * cw_ref[conv_w - 1:conv_w, :]
        for j in range(conv_w - 1):
            acc = acc + st_ref[j] * cw_ref[j:j + 1, :]
        outs.append(acc)
    act_ref[...] = (_silu(outs[0]) * outs[1]).astype(act_ref.dtype)


def _ffn_up_sample(xn, w_up, cw, cb, state):
    m, k = xn.shape
    n2 = w_up.shape[1]
    dff = n2 // 2
    tn = COL_TILE
    nj = dff // tn
    cwid = cw.shape[0]
    cb2 = cb.reshape(1, n2)
    st = state.transpose(1, 0, 2)
    act, ua, ub = pl.pallas_call(
        functools.partial(_ffn_up_sample_kernel, conv_w=cwid),
        grid=(nj,),
        in_specs=[
            pl.BlockSpec((m, k), lambda j: (0, 0)),
            pl.BlockSpec((k, tn), lambda j: (0, j)),
            pl.BlockSpec((k, tn), lambda j: (0, nj + j)),
            pl.BlockSpec((cwid, tn), lambda j: (0, j)),
            pl.BlockSpec((cwid, tn), lambda j: (0, nj + j)),
            pl.BlockSpec((1, tn), lambda j: (0, j)),
            pl.BlockSpec((1, tn), lambda j: (0, nj + j)),
            pl.BlockSpec((cwid - 1, m, tn), lambda j: (0, 0, j)),
            pl.BlockSpec((cwid - 1, m, tn), lambda j: (0, 0, nj + j)),
        ],
        out_specs=[pl.BlockSpec((m, tn), lambda j: (0, j))] * 3,
        out_shape=[
            jax.ShapeDtypeStruct((m, dff), BF16),
            jax.ShapeDtypeStruct((m, dff), F32),
            jax.ShapeDtypeStruct((m, dff), F32),
        ],
        compiler_params=_cparams(("parallel",)),
        name="ffn_up_sample",
    )(xn, w_up, w_up, cw, cw, cb2, cb2, st, st)
    up = jnp.concatenate([ua, ub], axis=-1)
    new_state = jnp.concatenate([state[:, 1:, :], up[:, None, :]], axis=1)
    return act, new_state


def _ple_kernel(x_ref, xt_ref, p_ref, g_ref, wp_ref, wg_ref, o_ref, xn_ref, *, row_chunk):
    @pl.when(pl.program_id(1) == 0)
    def _():
        tm = x_ref.shape[0]
        for c in range(tm // row_chunk):
            rows = pl.ds(c * row_chunk, row_chunk)
            xn_ref[rows, :] = _rms_rows(x_ref[rows, :], g_ref[...]).astype(BF16)

    emb = _dot(p_ref[...].astype(BF16), wp_ref[...])
    gate = _dot(xn_ref[...], wg_ref[...])
    o_ref[...] = xt_ref[...] + emb * jax.nn.sigmoid(gate)


def _ple(x, p, g, w_ple, w_gate):
    m, d = x.shape
    kp = p.shape[1]
    tm = min(ROW_TILE, m)
    tn = COL_TILE
    row_chunk = min(256, tm)
    return pl.pallas_call(
        functools.partial(_ple_kernel, row_chunk=row_chunk),
        grid=(m // tm, d // tn),
        in_specs=[
            pl.BlockSpec((tm, d), lambda i, j: (i, 0)),
            pl.BlockSpec((tm, tn), lambda i, j: (i, j)),
            pl.BlockSpec((tm, kp), lambda i, j: (i, 0)),
            pl.BlockSpec((1, d), lambda i, j: (0, 0)),
            pl.BlockSpec((kp, tn), lambda i, j: (0, j)),
            pl.BlockSpec((d, tn), lambda i, j: (0, j)),
        ],
        out_specs=pl.BlockSpec((tm, tn), lambda i, j: (i, j)),
        out_shape=jax.ShapeDtypeStruct((m, d), F32),
        scratch_shapes=[pltpu.VMEM((tm, d), BF16)],
        compiler_params=_cparams(("parallel", "arbitrary")),
        name="ple",
    )(x, x, p, g.reshape(1, d), w_ple, w_gate)


def kernel(x_prompt, x_sample, cache_k, cache_v, state_ssm, state_conv, state_ffn_conv, page_table,
           p_prompt, p_sample, g_mix, w_in, g_q, g_k, sb_bias, conv_w, conv_b, dt_bias, a_log, d_skip, g_ssm,
           w_proj_a, w_proj_b, w_out, g_ffn, w_up, ffn_conv_w, ffn_conv_b, w_down, w_ple, g_ple, w_ple_gate):
    depth = w_in.shape[0]
    assert depth == 1
    batch, seq, d_model = x_prompt.shape
    db, dseq, _ = x_sample.shape
    assert dseq == 1
    _, n_pool, page, sb_heads, hd = cache_k.shape
    sb_width = sb_heads * hd
    _, _, heads, pdim, nstate = state_ssm.shape
    inner = heads * pdim
    conv_dim = state_conv.shape[-1]
    groups = (conv_dim - inner) // (2 * nstate)
    dims = (heads, pdim, nstate, groups)

    off_z = 3 * sb_width
    off_dt = off_z + inner + conv_dim
    off_ga = off_dt + heads
    w_in_t = jnp.swapaxes(w_in, 1, 2)[0]
    assert off_dt % heads == 0 and COL_TILE % heads == 0
    tail_blocks = COL_TILE // heads
    tail_ga, tail_gb = COL_TILE, COL_TILE + d_model

    def tail_block(j, r):
        return jnp.where(j == 0, off_dt // heads + r, off_ga // heads + (j - 1) * tail_blocks + r)
    wup_b = w_up[0].astype(BF16)
    wa_b = w_proj_a[0].astype(BF16)
    wb_b = w_proj_b[0].astype(BF16)
    wout_b = w_out[0].astype(BF16)
    wdown_b = w_down[0].astype(BF16)
    wple_b = w_ple[0].astype(BF16)
    wgate_b = w_ple_gate[0].astype(BF16)
    gq = g_q[0].reshape(1, hd)
    gk = g_k[0].reshape(1, hd)

    def mixer_inputs(x2):
        xn = _rms_norm_bf16(x2, g_mix[0])
        def proj(col0, ncols, **kw):
            assert col0 % PROJ_COL_TILE == 0
            return _proj(xn, w_in_t, ncols=ncols, tn=PROJ_COL_TILE, w_rows=PROJ_COL_TILE,
                         w_block=lambda j, r: col0 // PROJ_COL_TILE + j, hd=hd, **kw)

        (qn,) = proj(0, sb_width, epilogue=functools.partial(_epi_q, hd=hd), extras=[gq],
                     out_kinds=["bf16"], name="proj_q")
        kn, knb = proj(sb_width, sb_width, epilogue=functools.partial(_epi_k, hd=hd), extras=[gk],
                       out_kinds=["heads", "bf16"], name="proj_k")
        v, vbt = proj(2 * sb_width, sb_width, epilogue=functools.partial(_epi_v, hd=hd), extras=[],
                      out_kinds=["heads", "bf16_t"], name="proj_v")
        (zx,) = proj(off_z, inner + conv_dim, epilogue=_epi_plain, extras=[], out_kinds=["f32"], name="proj_zx")
        (tail,) = _proj(xn, w_in_t, ncols=COL_TILE + 2 * d_model, tn=COL_TILE, w_rows=heads, w_block=tail_block,
                        epilogue=_epi_plain, extras=[], out_kinds=["f32"], hd=hd, name="proj_tail",
                        first_tile_rows=heads)
        return qn, kn, knb, v, vbt, zx, tail

    def merge_branches(x2, o_a, o_b, tail):
        merged = _merge(o_a, o_b, wa_b, wb_b, tail, off_ga=tail_ga, off_gb=tail_gb)
        return _resid_matmul_norm(x2, merged, wout_b, g_ffn[0])

    def ffn_down_and_embed(x1, act, p2):
        return _ple(_resid_matmul(x1, act, wdown_b), p2, g_ple[0], wple_b, wgate_b)

    xp2 = x_prompt.reshape(batch * seq, d_model)
    xs2 = x_sample.reshape(db, d_model)
    qn, kn, knb, v, vbt, zx, tail = mixer_inputs(xp2)
    qn_s, kn_s, _, v_s, _, zx_s, tail_s = mixer_inputs(xs2)
    n_pages = page_table.shape[1]

    o_a = _sb_prompt(qn, knb, vbt, sb_bias[0], batch=batch, seq=seq, heads=sb_heads, hd=hd)
    o_b, ssm_p, z_s = _ssd_prompt(zx, tail[:, :heads], conv_w[0], conv_b[0], dt_bias[0], a_log[0], d_skip[0],
                                  g_ssm[0], qn_s, cache_k, page_table, batch=batch, seq=seq, dims=dims)
    w_s = _sb_weights(z_s, sb_bias[0])
    cw = conv_w.shape[1]
    conv_p = zx.reshape(batch, seq, -1)[:, seq - (cw - 1):, inner:]
    x1, xn1 = merge_branches(xp2, o_a, o_b, tail)
    act, ffn_p, o_a_s = _ffn_up_prompt(xn1, wup_b, ffn_conv_w[0], ffn_conv_b[0], w_s, cache_v, page_table, seq=seq)
    yp = ffn_down_and_embed(x1, act, p_prompt[0].reshape(batch * seq, -1))
    k_p = kn.reshape(batch, seq, sb_heads, hd)
    v_p = v.reshape(batch, seq, sb_heads, hd)

    o_b_s, conv_s, ssm_s = _ssd_sample(
        zx_s[:, inner:], zx_s[:, :inner], tail_s[:, :heads], state_conv[0], state_ssm[0],
        conv_w[0], conv_b[0], dt_bias[0], a_log[0], d_skip[0], g_ssm[0], dims=dims)
    x1_s, xn1_s = merge_branches(xs2, o_a_s, o_b_s, tail_s)
    act_s, ffn_s = _ffn_up_sample(xn1_s, wup_b, ffn_conv_w[0], ffn_conv_b[0], state_ffn_conv[0])
    ys = ffn_down_and_embed(x1_s, act_s, p_sample[0].reshape(db, -1))

    return (yp.reshape(batch, seq, d_model), ys.reshape(db, 1, d_model),
            k_p[None], v_p[None], ssm_p[None], conv_p[None], ffn_p[None],
            kn_s.reshape(db, 1, sb_heads, hd)[None], v_s.reshape(db, 1, sb_heads, hd)[None],
            ssm_s[None], conv_s[None], ffn_s[None])
```

```python
import functools
import math

import jax
import jax.numpy as jnp
from jax import lax
from jax.experimental import pallas as pl
from jax.experimental.pallas import tpu as pltpu

F32 = jnp.float32
BF16 = jnp.bfloat16
EPS = 1e-6
LANES = 128
SUBLANES = 8
VMEM_LIMIT = 56 * 1024 * 1024
COL_TILE = 512
ROW_TILE = 1024
PROJ_COL_TILE = 1024
ATTN_BLOCK = 256
ATTN_HEADS_PER_STEP = 8
FFN_ROW_TILE = 512


def _cparams(sem):
    return pltpu.CompilerParams(dimension_semantics=sem, vmem_limit_bytes=VMEM_LIMIT)


def _dot(a, b):
    return jnp.dot(a, b, preferred_element_type=F32)


def _dot_nt(a, b):
    return lax.dot_general(a, b, (((1,), (1,)), ((), ())), preferred_element_type=F32)


def _split3(x):
    hi = x.astype(BF16)
    r = x - hi.astype(F32)
    mid = r.astype(BF16)
    lo = (r - mid.astype(F32)).astype(BF16)
    return hi, mid, lo


def _dot3(x, m):
    hi, mid, lo = _split3(x)
    return _dot(hi, m) + _dot(mid, m) + _dot(lo, m)


def _softplus(x):
    return jnp.maximum(x, 0.0) + jnp.log(1.0 + jnp.exp(-jnp.abs(x)))


LOG2E = math.log2(math.e)


def _softplus2(x):
    neg_abs = lax.bitcast_convert_type(lax.bitcast_convert_type(x, jnp.uint32) | jnp.uint32(0x80000000), F32)
    return jnp.maximum(x, 0.0) + jnp.log2(1.0 + jnp.exp2(neg_abs))


def _silu(x):
    return x * jax.nn.sigmoid(x)


def _rms_rows(x, g):
    ms = jnp.mean(x * x, axis=-1, keepdims=True)
    return x * lax.rsqrt(ms + EPS) * g


def _rms_norm_kernel(x_ref, g_ref, o_ref):
    o_ref[...] = _rms_rows(x_ref[...], g_ref[...]).astype(BF16)


def _rms_norm_bf16(x, g):
    m, k = x.shape
    tm = min(256, m)
    return pl.pallas_call(
        _rms_norm_kernel,
        grid=(m // tm,),
        in_specs=[pl.BlockSpec((tm, k), lambda i: (i, 0)), pl.BlockSpec((1, k), lambda i: (0, 0))],
        out_specs=pl.BlockSpec((tm, k), lambda i: (i, 0)),
        out_shape=jax.ShapeDtypeStruct((m, k), BF16),
        compiler_params=_cparams(("parallel",)),
        name="rms_norm",
    )(x, g.reshape(1, k))


def _proj_kernel(x_ref, *rest, n_w, n_extra, n_out, epilogue, first_tile_rows):
    w_refs = rest[:n_w]
    extras = rest[n_w:n_w + n_extra]
    outs = rest[n_w + n_extra:n_w + n_extra + n_out]
    wb_ref = rest[n_w + n_extra + n_out]

    @pl.when(pl.program_id(1) == 0)
    def _():
        w = jnp.concatenate([r[...] for r in w_refs], axis=0) if n_w > 1 else w_refs[0][...]
        if first_tile_rows is not None:
            row = lax.broadcasted_iota(jnp.int32, w.shape, 0)
            w = jnp.where((pl.program_id(0) > 0) | (row < first_tile_rows), w, 0.0)
        wb_ref[...] = w.astype(BF16)

    epilogue(_dot_nt(x_ref[...], wb_ref[...]), *extras, *outs)


def _epi_plain(acc, o_ref):
    o_ref[...] = acc


def _epi_q(acc, g_ref, qn_ref, *, hd):
    scale = LOG2E / math.sqrt(hd)
    for h in range(acc.shape[1] // hd):
        lanes = slice(h * hd, (h + 1) * hd)
        qn_ref[:, lanes] = (_rms_rows(acc[:, lanes], g_ref[...]) * scale).astype(BF16)


def _epi_k(acc, g_ref, kn_ref, knb_ref, *, hd):
    for h in range(acc.shape[1] // hd):
        lanes = slice(h * hd, (h + 1) * hd)
        kn = _rms_rows(acc[:, lanes], g_ref[...])
        kn_ref[:, h, :] = kn
        knb_ref[:, lanes] = kn.astype(BF16)


def _epi_v(acc, v_ref, vbt_ref, *, hd):
    for h in range(acc.shape[1] // hd):
        lanes = slice(h * hd, (h + 1) * hd)
        v_ref[:, h, :] = acc[:, lanes]
        vbt_ref[lanes, :] = acc[:, lanes].T.astype(BF16)


def _proj(xn, wt, *, ncols, tn, w_rows, w_block, epilogue, extras, out_kinds, hd, name, first_tile_rows=None):
    m, k = xn.shape
    tm = min(ROW_TILE, m)
    assert ncols % tn == 0 and tn % w_rows == 0
    n_w = tn // w_rows
    out_specs, out_shapes = [], []
    for kind in out_kinds:
        if kind == "heads":
            out_specs.append(pl.BlockSpec((tm, tn // hd, hd), lambda j, i: (i, j, 0)))
            out_shapes.append(jax.ShapeDtypeStruct((m, ncols // hd, hd), F32))
        elif kind == "bf16_t":
            out_specs.append(pl.BlockSpec((tn, tm), lambda j, i: (j, i)))
            out_shapes.append(jax.ShapeDtypeStruct((ncols, m), BF16))
        else:
            out_specs.append(pl.BlockSpec((tm, tn), lambda j, i: (i, j)))
            out_shapes.append(jax.ShapeDtypeStruct((m, ncols), F32 if kind == "f32" else BF16))
    extra_specs = [pl.BlockSpec(e.shape, lambda j, i: (0, 0)) for e in extras]

    def w_spec(r):
        return pl.BlockSpec((w_rows, k), lambda j, i: (w_block(j, r), 0))

    return pl.pallas_call(
        functools.partial(_proj_kernel, n_w=n_w, n_extra=len(extras), n_out=len(out_kinds), epilogue=epilogue,
                          first_tile_rows=first_tile_rows),
        grid=(ncols // tn, m // tm),
        in_specs=[pl.BlockSpec((tm, k), lambda j, i: (i, 0))] + [w_spec(r) for r in range(n_w)] + extra_specs,
        out_specs=out_specs,
        out_shape=out_shapes,
        scratch_shapes=[pltpu.VMEM((tn, k), BF16)],
        compiler_params=_cparams(("parallel", "arbitrary")),
        name=name,
    )(xn, *([wt] * n_w), *extras)


def _sb_prompt_kernel(bias_ref, q_ref, k_ref, vt_ref, tri_ref, o_ref, *, tq, tk, hd, hps):
    hb = pl.program_id(1)
    qi = pl.program_id(2)
    tri = tri_ref[...]

    def block(k0, state, masked):
        if masked:
            row = lax.broadcasted_iota(jnp.int32, (tq, tk), 0)
            col = lax.broadcasted_iota(jnp.int32, (tq, tk), 1)
            keep = col < row
        heads = range(hps)
        lanes = [slice(g * hd, (g + 1) * hd) for g in heads]
        zs = [_dot_nt(q_ref[:, lanes[g]], k_ref[pl.ds(k0, tk), lanes[g]]) + bias_ref[hb * hps + g] for g in heads]
        sps = [_softplus2(z) for z in zs]
        if masked:
            sps = [jnp.where(keep, sp, 0.0) for sp in sps]
        spb = [sp.astype(BF16) for sp in sps]
        tails = [_dot(sb, tri) for sb in spb]
        ws = [jnp.exp2(zs[g] - sps[g] - tails[g] - state[g][0]) for g in heads]
        if masked:
            ws = [jnp.where(keep, w, 0.0) for w in ws]
        pvs = [_dot_nt(vt_ref[lanes[g], pl.ds(k0, tk)], ws[g].astype(BF16)) for g in heads]
        sums = [tails[g][:, 0:1] + spb[g][:, 0:1].astype(F32) for g in heads]
        return tuple((state[g][0] + sums[g], state[g][1] + pvs[g]) for g in heads)

    state = tuple((jnp.zeros((tq, 1), F32), jnp.zeros((hd, tq), F32)) for _ in range(hps))
    state = block(pl.multiple_of(qi * tq, tq), state, True)

    def body(it, st):
        return block(pl.multiple_of((qi - 1 - it) * tk, tk), st, False)

    state = lax.fori_loop(0, qi, body, state)
    for g in range(hps):
        o_ref[:, g * hd:(g + 1) * hd] = state[g][1].T.astype(o_ref.dtype)


def _sb_prompt(qn, knb, vbt, sb_bias, *, batch, seq, heads, hd):
    tq = tk = min(ATTN_BLOCK, seq)
    hps = ATTN_HEADS_PER_STEP
    assert seq % tq == 0 and heads % hps == 0
    width = heads * hd
    q3 = qn.reshape(batch, seq, width)
    k3 = knb.reshape(batch, seq, width)
    tri = (jnp.arange(tk)[:, None] > jnp.arange(tk)[None, :]).astype(BF16)
    out = pl.pallas_call(
        functools.partial(_sb_prompt_kernel, tq=tq, tk=tk, hd=hd, hps=hps),
        grid=(batch, heads // hps, seq // tq),
        in_specs=[
            pl.BlockSpec(memory_space=pltpu.SMEM),
            pl.BlockSpec((None, tq, hps * hd), lambda b, h, i: (b, i, h)),
            pl.BlockSpec((None, seq, hps * hd), lambda b, h, i: (b, 0, h)),
            pl.BlockSpec((hps * hd, seq), lambda b, h, i: (h, b)),
            pl.BlockSpec((tk, tk), lambda b, h, i: (0, 0)),
        ],
        out_specs=pl.BlockSpec((None, tq, hps * hd), lambda b, h, i: (b, i, h)),
        out_shape=jax.ShapeDtypeStruct((batch, seq, width), BF16),
        compiler_params=_cparams(("parallel", "parallel", "arbitrary")),
        name="sb_attn_prompt",
    )(sb_bias.astype(F32) * LOG2E, q3, k3, vbt, tri)
    return out.reshape(batch * seq, width)


class _PageJob:
    def __init__(self, cache, page_table, n_steps):
        self.db, self.n_pages = page_table.shape
        _, _, self.page, self.heads, self.hd = cache.shape
        self.total = self.db * self.n_pages
        self.n_steps = n_steps
        self.pps = -(-self.total // n_steps)
        self.pps_pad = -(-self.pps // SUBLANES) * SUBLANES
        self.width = self.page * self.heads

    def page_specs(self, step_of):
        def spec(p):
            def index(*args):
                pt = args[-1]
                flat = jnp.minimum(step_of(*args[:-1]) * self.pps + p, self.total - 1)
                return (0, pt[flat], 0, 0, 0)
            return pl.BlockSpec((None, None, self.page, self.heads, self.hd), index)
        return [spec(p) for p in range(self.pps)]

    def sequence_of(self, step, p):
        return jnp.minimum(step * self.pps + p, self.total - 1) // self.n_pages

    def rows_to_steps(self, rows):
        rows = jnp.pad(rows, ((0, self.n_steps * self.pps - self.total), (0, 0)))
        rows = rows.reshape(self.n_steps, self.pps, self.width)
        return jnp.pad(rows, ((0, 0), (0, self.pps_pad - self.pps), (0, 0)))

    def steps_to_rows(self, blocks):
        return blocks[:, :self.pps].reshape(self.n_steps * self.pps, self.width)[:self.total]


def _own_head(heads, width):
    row = lax.broadcasted_iota(jnp.int32, (heads, width), 0)
    col = lax.broadcasted_iota(jnp.int32, (heads, width), 1)
    return col % heads == row


def _flat_page(page_ref, heads):
    kp = page_ref[...]
    return kp.reshape(kp.shape[0] * heads, kp.shape[2]).astype(BF16)


def _score_page(job, step, p, q_ref, k_refs):
    q = q_ref[job.sequence_of(step, p)]
    full = _dot_nt(q, _flat_page(k_refs[p], job.heads))
    return jnp.sum(jnp.where(_own_head(job.heads, job.width), full, 0.0), axis=0, keepdims=True)


def _store_scores(job, rows, z_ref):
    if job.pps_pad > job.pps:
        rows = rows + [jnp.zeros((job.pps_pad - job.pps, job.width), F32)]
    z_ref[...] = jnp.concatenate(rows, axis=0)


def _sum_weighted_pages(job, step, w_ref, v_refs, acc_ref):
    own = _own_head(job.heads, job.width)
    w_all = w_ref[...]
    for p in range(job.pps):
        wm = jnp.where(own, jnp.broadcast_to(w_all[p:p + 1, :], (job.heads, job.width)), 0.0).astype(BF16)
        b = job.sequence_of(step, p)
        acc_ref[b] = acc_ref[b] + _dot(wm, _flat_page(v_refs[p], job.heads))


def _sb_weights_kernel(z_ref, bias_ref, w_ref, *, heads):
    n_pages, width = z_ref.shape
    lane = lax.broadcasted_iota(jnp.int32, (n_pages, width), 1)
    row = lax.broadcasted_iota(jnp.int32, (n_pages, width), 0)
    z = z_ref[...] + bias_ref[...]
    sp = _softplus2(z)

    incl = sp
    shift = heads
    while shift < width:
        moved = pltpu.roll(incl, width - shift, 1)
        incl = incl + jnp.where(lane < width - shift, moved, 0.0)
        shift *= 2
    total = jnp.where(lane < heads, incl, 0.0)
    shift = heads
    while shift < width:
        total = total + pltpu.roll(total, shift, 1)
        shift *= 2
    below = total
    shift = 1
    while shift < n_pages:
        moved = pltpu.roll(below, n_pages - shift, 0)
        below = below + jnp.where(row < n_pages - shift, moved, 0.0)
        shift *= 2
    later = (incl - sp) + (below - total)
    w_ref[...] = jnp.exp2(z - sp - later)


def _sb_weights(z, sb_bias):
    db, n_pages, width = z.shape
    heads = sb_bias.shape[0]
    assert width % heads == 0 and (width // heads) & (width // heads - 1) == 0 and n_pages & (n_pages - 1) == 0
    bias = jnp.tile(sb_bias.astype(F32) * LOG2E, width // heads).reshape(1, width)
    return pl.pallas_call(
        functools.partial(_sb_weights_kernel, heads=heads),
        grid=(db,),
        in_specs=[
            pl.BlockSpec((None, n_pages, width), lambda b: (b, 0, 0)),
            pl.BlockSpec((1, width), lambda b: (0, 0)),
        ],
        out_specs=pl.BlockSpec((None, n_pages, width), lambda b: (b, 0, 0)),
        out_shape=jax.ShapeDtypeStruct((db, n_pages, width), F32),
        compiler_params=_cparams(("parallel",)),
        name="sb_sample_weights",
    )(z, bias)


def _ssd_prompt_kernel(pt_ref, x_ref, bm_ref, cm_ref, z_ref, dt_ref,
                       wx_ref, wb_ref, wc_ref, bx_ref, bb_ref, bc_ref,
                       dtb_ref, al_ref, dskip_ref, gssm_ref, triu_ref, q_ref, *rest,
                       chunk, hpg, pdim, nstate, conv_w, job):
    k_refs = rest[:job.pps]
    o_ref, st_ref, sc_ref, ext_ref, ht_ref = rest[job.pps:]
    c = pl.program_id(2)
    gw = hpg * pdim
    halo = SUBLANES
    widths = (gw, nstate, nstate)
    offs = (0, gw, gw + nstate)

    @pl.when(c == 0)
    def _():
        ext_ref[0:halo, :] = jnp.zeros((halo, ext_ref.shape[1]), F32)
        ht_ref[...] = jnp.zeros_like(ht_ref)

    raws = (x_ref, bm_ref, cm_ref)
    wrefs = (wx_ref, wb_ref, wc_ref)
    brefs = (bx_ref, bb_ref, bc_ref)
    conv = []
    for raw, wref, bref, off, wd in zip(raws, wrefs, brefs, offs, widths):
        ext_ref[halo:halo + chunk, off:off + wd] = raw[...]
        acc = jnp.broadcast_to(bref[...], (chunk, wd))
        for j in range(conv_w):
            start = halo - (conv_w - 1) + j
            acc = acc + ext_ref[start:start + chunk, off:off + wd] * wref[j:j + 1, :]
        conv.append(_silu(acc))
    ext_ref[0:halo, :] = ext_ref[chunk:chunk + halo, :]
    xs, bmat, cmat = conv

    dt_r = _softplus(dt_ref[...] + dtb_ref[...])
    a_r = -jnp.exp(al_ref[...])
    da_r = _dot3(dt_r * a_r, triu_ref[...])
    dt_c = dt_r.T
    da_c = da_r.T

    bmat_b = bmat.astype(BF16)
    cmat_b = cmat.astype(BF16)
    row = lax.broadcasted_iota(jnp.int32, (chunk, chunk), 0)
    col = lax.broadcasted_iota(jnp.int32, (chunk, chunk), 1)
    causal = col <= row
    cb = jnp.where(causal, _dot_nt(cmat_b, bmat_b), 0.0)
    bmat_t = bmat.T.astype(BF16)
    first = lax.broadcasted_iota(jnp.int32, (chunk, LANES), 1) < pdim

    heads_per_tile = LANES // pdim
    assert heads_per_tile == 2
    n_tiles = hpg // heads_per_tile
    step = (pl.program_id(0) * pl.num_programs(1) + pl.program_id(1)) * pl.num_programs(2) + c
    scores = []
    us = []
    ssq = jnp.zeros((chunk, 1), F32)
    for t in range(n_tiles):
        for p in range(t * job.pps // n_tiles, (t + 1) * job.pps // n_tiles):
            scores.append(_score_page(job, step, p, q_ref, k_refs))
        r0 = t * heads_per_tile
        lanes = slice(t * LANES, (t + 1) * LANES)
        da_bc = [jnp.broadcast_to(da_c[:, r:r + 1], (chunk, LANES)) for r in (r0, r0 + 1)]
        dt_bc = [jnp.broadcast_to(dt_c[:, r:r + 1], (chunk, LANES)) for r in (r0, r0 + 1)]
        da_pair = jnp.where(first, da_bc[0], da_bc[1])
        dt_pair = jnp.where(first, dt_bc[0], dt_bc[1])
        last_pair = da_pair[chunk - 1:chunk, :]
        decay = []
        for i, r in enumerate((r0, r0 + 1)):
            diff = da_bc[i] - da_r[r:r + 1, :]
            decay.append((cb * jnp.exp(jnp.minimum(diff, 0.0))).astype(BF16))
        m_cat = jnp.concatenate(decay, axis=1)
        xp = xs[:, lanes]
        xdt = xp * dt_pair
        xbd = jnp.concatenate([jnp.where(first, xdt, 0.0), jnp.where(first, 0.0, xdt)], axis=0).astype(BF16)
        y = _dot(m_cat, xbd)
        ht_old = ht_ref[:, lanes]
        y = y + _dot(cmat_b, ht_old.astype(BF16)) * jnp.exp(da_pair)
        xw = (xdt * jnp.exp(last_pair - da_pair)).astype(BF16)
        ht_ref[:, lanes] = ht_old * jnp.exp(last_pair) + _dot(bmat_t, xw)
        y = y + xp * dskip_ref[:, lanes]
        u = y * _silu(z_ref[:, lanes])
        ssq = ssq + jnp.sum(u * u, axis=1, keepdims=True)
        us.append(u)
    inv = lax.rsqrt(ssq / gw + EPS)
    for t, u in enumerate(us):
        lanes = slice(t * LANES, (t + 1) * LANES)
        o_ref[:, lanes] = (u * inv * gssm_ref[:, lanes]).astype(o_ref.dtype)
    _store_scores(job, scores, sc_ref)

    @pl.when(c == pl.num_programs(2) - 1)
    def _():
        st_ref[...] = ht_ref[...]


def _ssd_prompt(zx, dt_raw, conv_w, conv_b, dt_bias, a_log, d_skip, g_ssm, qn_s, cache_k, page_table, *,
                batch, seq, dims):
    heads, pdim, nstate, groups = dims
    hpg = heads // groups
    gw = hpg * pdim
    inner = heads * pdim
    chunk = min(LANES, seq)
    n_chunks = seq // chunk
    cw = conv_w.shape[0]
    zx3 = zx.reshape(batch, seq, zx.shape[1])
    dtt = dt_raw.T
    idx = jnp.arange(chunk)
    triu = (idx[:, None] <= idx[None, :]).astype(BF16)
    job = _PageJob(cache_k, page_table, batch * groups * n_chunks)

    zcol0 = 0
    xcol0 = inner // gw
    bcol0 = (2 * inner) // nstate
    ccol0 = (2 * inner + groups * nstate) // nstate
    wb0 = inner // nstate
    wc0 = (inner + groups * nstate) // nstate

    def step_of(b, g, c):
        return (b * groups + g) * n_chunks + c

    def tok(width, col0):
        return pl.BlockSpec((None, chunk, width), lambda b, g, c, pt: (b, c, col0 + g))

    def par(rows, width, col0):
        return pl.BlockSpec((rows, width), lambda b, g, c, pt: (0, col0 + g))

    def grp(shape):
        return pl.BlockSpec((None,) + shape, lambda b, g, c, pt: (g, 0, 0))

    conv_b2 = conv_b.reshape(1, -1)
    o_b, st, scores = pl.pallas_call(
        functools.partial(_ssd_prompt_kernel, chunk=chunk, hpg=hpg, pdim=pdim, nstate=nstate, conv_w=cw, job=job),
        grid_spec=pltpu.PrefetchScalarGridSpec(
            num_scalar_prefetch=1,
            grid=(batch, groups, n_chunks),
            in_specs=[
                tok(gw, xcol0), tok(nstate, bcol0), tok(nstate, ccol0), tok(gw, zcol0),
                pl.BlockSpec((hpg, chunk), lambda b, g, c, pt: (g, b * n_chunks + c)),
                par(cw, gw, 0), par(cw, nstate, wb0), par(cw, nstate, wc0),
                par(1, gw, 0), par(1, nstate, wb0), par(1, nstate, wc0),
                grp((hpg, 1)), grp((hpg, 1)),
                par(1, gw, 0), par(1, gw, 0),
                pl.BlockSpec((chunk, chunk), lambda b, g, c, pt: (0, 0)),
                pl.BlockSpec((job.db, job.heads, job.hd), lambda b, g, c, pt: (0, 0, 0)),
            ] + job.page_specs(step_of),
            out_specs=[
                pl.BlockSpec((None, chunk, gw), lambda b, g, c, pt: (b, c, g)),
                pl.BlockSpec((None, None, nstate, gw), lambda b, g, c, pt: (b, g, 0, 0)),
                pl.BlockSpec((None, job.pps_pad, job.width), lambda b, g, c, pt: (step_of(b, g, c), 0, 0)),
            ],
            scratch_shapes=[
                pltpu.VMEM((chunk + SUBLANES, gw + 2 * nstate), F32),
                pltpu.VMEM((nstate, gw), F32),
            ],
        ),
        out_shape=[
            jax.ShapeDtypeStruct((batch, seq, inner), BF16),
            jax.ShapeDtypeStruct((batch, groups, nstate, gw), F32),
            jax.ShapeDtypeStruct((job.n_steps, job.pps_pad, job.width), F32),
        ],
        compiler_params=_cparams(("parallel", "parallel", "arbitrary")),
        name="ssd_prompt",
    )(page_table.reshape(-1), zx3, zx3, zx3, zx3, dtt,
      conv_w, conv_w, conv_w, conv_b2, conv_b2, conv_b2,
      dt_bias.reshape(groups, hpg, 1), a_log.reshape(groups, hpg, 1),
      jnp.repeat(d_skip, pdim).reshape(1, inner), g_ssm.reshape(1, inner),
      triu, qn_s.reshape(job.db, job.heads, job.hd), *([cache_k] * job.pps))
    st = st.reshape(batch, groups, nstate, hpg, pdim).transpose(0, 1, 3, 4, 2)
    scores = job.steps_to_rows(scores).reshape(job.db, job.n_pages, job.width)
    return o_b.reshape(batch * seq, inner), st.reshape(batch, heads, pdim, nstate), scores


def _ssd_sample_kernel(xbc_ref, z_ref, dt_ref, cst_ref, h0_ref,
                       cw_ref, cb_ref, dtb_ref, al_ref, dskip_ref, gssm_ref, expand_ref,
                       o_ref, cnew_ref, h_ref, *, inner, nstate, groups, conv_w):
    gw = inner // groups
    x = xbc_ref[...]
    acc = cb_ref[...] + x * cw_ref[conv_w - 1:conv_w, :]
    for j in range(conv_w - 1):
        acc = acc + cst_ref[j:j + 1, :] * cw_ref[j:j + 1, :]
        if j > 0:
            cnew_ref[j - 1:j, :] = cst_ref[j:j + 1, :]
    cnew_ref[conv_w - 2:conv_w - 1, :] = x
    xc = _silu(acc)
    xs = xc[:, :inner]
    bv = xc[:, inner:inner + groups * nstate]
    cv = xc[:, inner + groups * nstate:]

    dt8 = jnp.broadcast_to(dt_ref[...], (SUBLANES, dt_ref.shape[1]))
    dt = _softplus(_dot3(dt8, expand_ref[...])[0:1, :] + dtb_ref[...])
    dec = jnp.exp(dt * (-jnp.exp(al_ref[...])))
    xdt = xs * dt
    row8 = lax.broadcasted_iota(jnp.int32, (SUBLANES, gw), 0)
    ys = []
    for g in range(groups):
        lanes = slice(g * gw, (g + 1) * gw)
        glanes = slice(g * nstate, (g + 1) * nstate)
        stacked = jnp.where(row8 == 0, xdt[:, lanes], jnp.where(row8 == 1, dec[:, lanes], 0.0))
        cols = stacked.T
        hn = h0_ref[lanes, :] * cols[:, 1:2] + cols[:, 0:1] * bv[:, glanes]
        h_ref[lanes, :] = hn
        cg = jnp.broadcast_to(cv[:, glanes], (SUBLANES, nstate)).astype(BF16)
        ys.append(_dot_nt(cg, hn.astype(BF16))[0:1, :])
    y = jnp.concatenate(ys, axis=1) + xs * dskip_ref[...]
    u = y * _silu(z_ref[...])
    for g in range(groups):
        lanes = slice(g * gw, (g + 1) * gw)
        ug = u[:, lanes]
        inv = lax.rsqrt(jnp.mean(ug * ug, axis=1, keepdims=True) + EPS)
        o_ref[:, lanes] = (ug * inv * gssm_ref[:, lanes]).astype(o_ref.dtype)


def _ssd_sample(xbc, z, dt_raw, state_conv, state_ssm, conv_w, conv_b, dt_bias, a_log, d_skip, g_ssm, *, dims):
    heads, pdim, nstate, groups = dims
    inner = heads * pdim
    db, conv_dim = xbc.shape
    cw = conv_w.shape[0]
    rep = lambda v: jnp.repeat(v.astype(F32), pdim).reshape(1, inner)
    expand = (jnp.arange(heads)[:, None] == (jnp.arange(inner)[None, :] // pdim)).astype(BF16)
    row = lambda width: pl.BlockSpec((None, 1, width), lambda b: (b, 0, 0))
    full = lambda shape: pl.BlockSpec(shape, lambda b: (0,) * len(shape))
    o_b, conv_new, h_new = pl.pallas_call(
        functools.partial(_ssd_sample_kernel, inner=inner, nstate=nstate, groups=groups, conv_w=cw),
        grid=(db,),
        in_specs=[
            row(conv_dim), row(inner), row(heads),
            pl.BlockSpec((None, cw - 1, conv_dim), lambda b: (b, 0, 0)),
            pl.BlockSpec((None, inner, nstate), lambda b: (b, 0, 0)),
            full((cw, conv_dim)), full((1, conv_dim)),
            full((1, inner)), full((1, inner)), full((1, inner)), full((1, inner)),
            full((heads, inner)),
        ],
        out_specs=[
            row(inner),
            pl.BlockSpec((None, cw - 1, conv_dim), lambda b: (b, 0, 0)),
            pl.BlockSpec((None, inner, nstate), lambda b: (b, 0, 0)),
        ],
        out_shape=[
            jax.ShapeDtypeStruct((db, 1, inner), BF16),
            jax.ShapeDtypeStruct((db, cw - 1, conv_dim), F32),
            jax.ShapeDtypeStruct((db, inner, nstate), F32),
        ],
        compiler_params=_cparams(("parallel",)),
        name="ssd_sample",
    )(xbc.reshape(db, 1, conv_dim), z.reshape(db, 1, inner), dt_raw.reshape(db, 1, heads),
      state_conv, state_ssm.reshape(db, inner, nstate),
      conv_w, conv_b.reshape(1, conv_dim), rep(dt_bias), rep(a_log), rep(d_skip), g_ssm.reshape(1, inner),
      expand)
    return o_b.reshape(db, inner), conv_new, h_new.reshape(db, heads, pdim, nstate)


def _merge_kernel(oa_ref, ob_ref, wa_ref, wb_ref, ga_ref, gb_ref, o_ref):
    a = _dot(oa_ref[...], wa_ref[...])
    b = _dot(ob_ref[...], wb_ref[...])
    o_ref[...] = (jax.nn.sigmoid(ga_ref[...]) * a + jax.nn.sigmoid(gb_ref[...]) * b).astype(o_ref.dtype)


def _merge(o_a, o_b, wa, wb, gates, *, off_ga, off_gb):
    m, ka = o_a.shape
    kb = o_b.shape[1]
    n = wa.shape[1]
    tm = min(ROW_TILE, m)
    tn = COL_TILE
    ga0, gb0 = off_ga // tn, off_gb // tn
    return pl.pallas_call(
        _merge_kernel,
        grid=(m // tm, n // tn),
        in_specs=[
            pl.BlockSpec((tm, ka), lambda i, j: (i, 0)),
            pl.BlockSpec((tm, kb), lambda i, j: (i, 0)),
            pl.BlockSpec((ka, tn), lambda i, j: (0, j)),
            pl.BlockSpec((kb, tn), lambda i, j: (0, j)),
            pl.BlockSpec((tm, tn), lambda i, j: (i, ga0 + j)),
            pl.BlockSpec((tm, tn), lambda i, j: (i, gb0 + j)),
        ],
        out_specs=pl.BlockSpec((tm, tn), lambda i, j: (i, j)),
        out_shape=jax.ShapeDtypeStruct((m, n), BF16),
        compiler_params=_cparams(("parallel", "arbitrary")),
        name="merge",
    )(o_a, o_b, wa, wb, gates, gates)


def _resid_mm_kernel(x_ref, a_ref, w_ref, g_ref, o_ref, on_ref):
    y = x_ref[...] + _dot(a_ref[...], w_ref[...])
    o_ref[...] = y
    on_ref[...] = _rms_rows(y, g_ref[...]).astype(BF16)


def _resid_matmul_norm(x, a, w, g):
    m, n = x.shape
    k = a.shape[1]
    tm = min(256, m)
    return pl.pallas_call(
        _resid_mm_kernel,
        grid=(m // tm,),
        in_specs=[
            pl.BlockSpec((tm, n), lambda i: (i, 0)),
            pl.BlockSpec((tm, k), lambda i: (i, 0)),
            pl.BlockSpec((k, n), lambda i: (0, 0)),
            pl.BlockSpec((1, n), lambda i: (0, 0)),
        ],
        out_specs=[pl.BlockSpec((tm, n), lambda i: (i, 0)), pl.BlockSpec((tm, n), lambda i: (i, 0))],
        out_shape=[jax.ShapeDtypeStruct((m, n), F32), jax.ShapeDtypeStruct((m, n), BF16)],
        compiler_params=_cparams(("parallel",)),
        name="resid_matmul_norm",
    )(x, a, w, g.reshape(1, n))


def _resid_mm_tiled_kernel(x_ref, a_ref, w_ref, o_ref):
    o_ref[...] = x_ref[...] + _dot(a_ref[...], w_ref[...])


def _resid_matmul(x, a, w):
    m, n = x.shape
    k = a.shape[1]
    tm = min(ROW_TILE, m)
    tn = COL_TILE
    return pl.pallas_call(
        _resid_mm_tiled_kernel,
        grid=(m // tm, n // tn),
        in_specs=[
            pl.BlockSpec((tm, tn), lambda i, j: (i, j)),
            pl.BlockSpec((tm, k), lambda i, j: (i, 0)),
            pl.BlockSpec((k, tn), lambda i, j: (0, j)),
        ],
        out_specs=pl.BlockSpec((tm, tn), lambda i, j: (i, j)),
        out_shape=jax.ShapeDtypeStruct((m, n), F32),
        compiler_params=_cparams(("parallel", "arbitrary")),
        name="resid_matmul",
    )(x, a, w)


def _ffn_up_prompt_kernel(pt_ref, xn_ref, wa_ref, wb_ref, cwa_ref, cwb_ref, cba_ref, cbb_ref, pw_ref, *rest,
                          seq, conv_w, job):
    v_refs = rest[:job.pps]
    act_ref, ta_ref, tb_ref, oa_ref, exta_ref, extb_ref, acc_ref = rest[job.pps:]
    i = pl.program_id(1)
    step = pl.program_id(0) * pl.num_programs(1) + i
    tm = xn_ref.shape[0]
    halo = SUBLANES

    @pl.when(step == 0)
    def _():
        acc_ref[...] = jnp.zeros_like(acc_ref)

    xn = xn_ref[...]
    start_of_seq = (i * tm) % seq == 0
    outs = []
    for w_ref, cw_ref, cb_ref, ext_ref, tail_ref in ((wa_ref, cwa_ref, cba_ref, exta_ref, ta_ref),
                                                     (wb_ref, cwb_ref, cbb_ref, extb_ref, tb_ref)):
        up = _dot(xn, w_ref[...])
        history = ext_ref[0:halo, :]
        ext_ref[0:halo, :] = jnp.where(start_of_seq, jnp.zeros_like(history), history)
        ext_ref[halo:halo + tm, :] = up
        acc = jnp.broadcast_to(cb_ref[...], up.shape)
        for j in range(conv_w):
            start = halo - (conv_w - 1) + j
            acc = acc + ext_ref[start:start + tm, :] * cw_ref[j:j + 1, :]
        tail = ext_ref[tm:tm + halo, :]
        ext_ref[0:halo, :] = tail
        tail_ref[...] = tail
        outs.append(acc)
    act_ref[...] = (_silu(outs[0]) * outs[1]).astype(act_ref.dtype)

    _sum_weighted_pages(job, step, pw_ref, v_refs, acc_ref)

    @pl.when(step == job.n_steps - 1)
    def _():
        oa_ref[...] = acc_ref[...].astype(oa_ref.dtype)


def _ffn_up_prompt(xn, w_up, cw, cb, w_s, cache_v, page_table, *, seq):
    m, k = xn.shape
    n2 = w_up.shape[1]
    dff = n2 // 2
    tm = min(FFN_ROW_TILE, m, seq)
    tn = COL_TILE
    nj = dff // tn
    ni = m // tm
    cwid = cw.shape[0]
    cb2 = cb.reshape(1, n2)
    job = _PageJob(cache_v, page_table, nj * ni)
    w_steps = job.rows_to_steps(w_s.reshape(job.total, job.width))

    def step_of(j, i):
        return j * ni + i

    act, ta, tb, o_a = pl.pallas_call(
        functools.partial(_ffn_up_prompt_kernel, seq=seq, conv_w=cwid, job=job),
        grid_spec=pltpu.PrefetchScalarGridSpec(
            num_scalar_prefetch=1,
            grid=(nj, ni),
            in_specs=[
                pl.BlockSpec((tm, k), lambda j, i, pt: (i, 0)),
                pl.BlockSpec((k, tn), lambda j, i, pt: (0, j)),
                pl.BlockSpec((k, tn), lambda j, i, pt: (0, nj + j)),
                pl.BlockSpec((cwid, tn), lambda j, i, pt: (0, j)),
                pl.BlockSpec((cwid, tn), lambda j, i, pt: (0, nj + j)),
                pl.BlockSpec((1, tn), lambda j, i, pt: (0, j)),
                pl.BlockSpec((1, tn), lambda j, i, pt: (0, nj + j)),
                pl.BlockSpec((None, job.pps_pad, job.width), lambda j, i, pt: (step_of(j, i), 0, 0)),
            ] + job.page_specs(step_of),
            out_specs=[
                pl.BlockSpec((tm, tn), lambda j, i, pt: (i, j)),
                pl.BlockSpec((None, SUBLANES, tn), lambda j, i, pt: (i, 0, j)),
                pl.BlockSpec((None, SUBLANES, tn), lambda j, i, pt: (i, 0, j)),
                pl.BlockSpec((job.db, job.heads, job.hd), lambda j, i, pt: (0, 0, 0)),
            ],
            scratch_shapes=[pltpu.VMEM((tm + SUBLANES, tn), F32), pltpu.VMEM((tm + SUBLANES, tn), F32),
                            pltpu.VMEM((job.db, job.heads, job.hd), F32)],
        ),
        out_shape=[
            jax.ShapeDtypeStruct((m, dff), BF16),
            jax.ShapeDtypeStruct((ni, SUBLANES, dff), F32),
            jax.ShapeDtypeStruct((ni, SUBLANES, dff), F32),
            jax.ShapeDtypeStruct((job.db, job.heads, job.hd), BF16),
        ],
        compiler_params=_cparams(("arbitrary", "arbitrary")),
        name="ffn_up_prompt",
    )(page_table.reshape(-1), xn, w_up, w_up, cw, cw, cb2, cb2, w_steps, *([cache_v] * job.pps))
    per_seq = seq // tm
    tails = jnp.concatenate([ta, tb], axis=-1)[per_seq - 1::per_seq, SUBLANES - (cwid - 1):, :]
    return act, tails, o_a.reshape(job.db, job.heads * job.hd)


def _ffn_up_sample_kernel(xn_ref, wa_ref, wb_ref, cwa_ref, cwb_ref, cba_ref, cbb_ref, sta_ref, stb_ref,
                          act_ref, ua_ref, ub_ref, *, conv_w):
    xn = xn_ref[...]
    outs = []
    for w_ref, cw_ref, cb_ref, st_ref, u_ref in ((wa_ref, cwa_ref, cba_ref, sta_ref, ua_ref),
                                                 (wb_ref, cwb_ref, cbb_ref, stb_ref, ub_ref)):
        up = _dot(xn, w_ref[...])
        u_ref[...] = up
        acc = cb_ref[...] + up * cw_ref[conv_w - 1:conv_w, :]
        for j in range(conv_w - 1):
            acc = acc + st_ref[j] * cw_ref[j:j + 1, :]
        outs.append(acc)
    act_ref[...] = (_silu(outs[0]) * outs[1]).astype(act_ref.dtype)


def _ffn_up_sample(xn, w_up, cw, cb, state):
    m, k = xn.shape
    n2 = w_up.shape[1]
    dff = n2 // 2
    tn = COL_TILE
    nj = dff // tn
    cwid = cw.shape[0]
    cb2 = cb.reshape(1, n2)
    st = state.transpose(1, 0, 2)
    act, ua, ub = pl.pallas_call(
        functools.partial(_ffn_up_sample_kernel, conv_w=cwid),
        grid=(nj,),
        in_specs=[
            pl.BlockSpec((m, k), lambda j: (0, 0)),
            pl.BlockSpec((k, tn), lambda j: (0, j)),
            pl.BlockSpec((k, tn), lambda j: (0, nj + j)),
            pl.BlockSpec((cwid, tn), lambda j: (0, j)),
            pl.BlockSpec((cwid, tn), lambda j: (0, nj + j)),
            pl.BlockSpec((1, tn), lambda j: (0, j)),
            pl.BlockSpec((1, tn), lambda j: (0, nj + j)),
            pl.BlockSpec((cwid - 1, m, tn), lambda j: (0, 0, j)),
            pl.BlockSpec((cwid - 1, m, tn), lambda j: (0, 0, nj + j)),
        ],
        out_specs=[pl.BlockSpec((m, tn), lambda j: (0, j))] * 3,
        out_shape=[
            jax.ShapeDtypeStruct((m, dff), BF16),
            jax.ShapeDtypeStruct((m, dff), F32),
            jax.ShapeDtypeStruct((m, dff), F32),
        ],
        compiler_params=_cparams(("parallel",)),
        name="ffn_up_sample",
    )(xn, w_up, w_up, cw, cw, cb2, cb2, st, st)
    up = jnp.concatenate([ua, ub], axis=-1)
    new_state = jnp.concatenate([state[:, 1:, :], up[:, None, :]], axis=1)
    return act, new_state


def _ple_kernel(x_ref, xt_ref, p_ref, g_ref, wp_ref, wg_ref, o_ref, xn_ref, *, row_chunk):
    @pl.when(pl.program_id(1) == 0)
    def _():
        tm = x_ref.shape[0]
        for c in range(tm // row_chunk):
            rows = pl.ds(c * row_chunk, row_chunk)
            xn_ref[rows, :] = _rms_rows(x_ref[rows, :], g_ref[...]).astype(BF16)

    emb = _dot(p_ref[...].astype(BF16), wp_ref[...])
    gate = _dot(xn_ref[...], wg_ref[...])
    o_ref[...] = xt_ref[...] + emb * jax.nn.sigmoid(gate)


def _ple(x, p, g, w_ple, w_gate):
    m, d = x.shape
    kp = p.shape[1]
    tm = min(ROW_TILE, m)
    tn = COL_TILE
    row_chunk = min(256, tm)
    return pl.pallas_call(
        functools.partial(_ple_kernel, row_chunk=row_chunk),
        grid=(m // tm, d // tn),
        in_specs=[
            pl.BlockSpec((tm, d), lambda i, j: (i, 0)),
            pl.BlockSpec((tm, tn), lambda i, j: (i, j)),
            pl.BlockSpec((tm, kp), lambda i, j: (i, 0)),
            pl.BlockSpec((1, d), lambda i, j: (0, 0)),
            pl.BlockSpec((kp, tn), lambda i, j: (0, j)),
            pl.BlockSpec((d, tn), lambda i, j: (0, j)),
        ],
        out_specs=pl.BlockSpec((tm, tn), lambda i, j: (i, j)),
        out_shape=jax.ShapeDtypeStruct((m, d), F32),
        scratch_shapes=[pltpu.VMEM((tm, d), BF16)],
        compiler_params=_cparams(("parallel", "arbitrary")),
        name="ple",
    )(x, x, p, g.reshape(1, d), w_ple, w_gate)


def kernel(x_prompt, x_sample, cache_k, cache_v, state_ssm, state_conv, state_ffn_conv, page_table,
           p_prompt, p_sample, g_mix, w_in, g_q, g_k, sb_bias, conv_w, conv_b, dt_bias, a_log, d_skip, g_ssm,
           w_proj_a, w_proj_b, w_out, g_ffn, w_up, ffn_conv_w, ffn_conv_b, w_down, w_ple, g_ple, w_ple_gate):
    depth = w_in.shape[0]
    assert depth == 1
    batch, seq, d_model = x_prompt.shape
    db, dseq, _ = x_sample.shape
    assert dseq == 1
    _, n_pool, page, sb_heads, hd = cache_k.shape
    sb_width = sb_heads * hd
    _, _, heads, pdim, nstate = state_ssm.shape
    inner = heads * pdim
    conv_dim = state_conv.shape[-1]
    groups = (conv_dim - inner) // (2 * nstate)
    dims = (heads, pdim, nstate, groups)

    off_z = 3 * sb_width
    off_dt = off_z + inner + conv_dim
    off_ga = off_dt + heads
    w_in_t = jnp.swapaxes(w_in, 1, 2)[0]
    assert off_dt % heads == 0 and COL_TILE % heads == 0
    tail_blocks = COL_TILE // heads
    tail_ga, tail_gb = COL_TILE, COL_TILE + d_model

    def tail_block(j, r):
        return jnp.where(j == 0, off_dt // heads + r, off_ga // heads + (j - 1) * tail_blocks + r)
    wup_b = w_up[0].astype(BF16)
    wa_b = w_proj_a[0].astype(BF16)
    wb_b = w_proj_b[0].astype(BF16)
    wout_b = w_out[0].astype(BF16)
    wdown_b = w_down[0].astype(BF16)
    wple_b = w_ple[0].astype(BF16)
    wgate_b = w_ple_gate[0].astype(BF16)
    gq = g_q[0].reshape(1, hd)
    gk = g_k[0].reshape(1, hd)

    def mixer_inputs(x2):
        xn = _rms_norm_bf16(x2, g_mix[0])
        def proj(col0, ncols, **kw):
            assert col0 % PROJ_COL_TILE == 0
            return _proj(xn, w_in_t, ncols=ncols, tn=PROJ_COL_TILE, w_rows=PROJ_COL_TILE,
                         w_block=lambda j, r: col0 // PROJ_COL_TILE + j, hd=hd, **kw)

        (qn,) = proj(0, sb_width, epilogue=functools.partial(_epi_q, hd=hd), extras=[gq],
                     out_kinds=["bf16"], name="proj_q")
        kn, knb = proj(sb_width, sb_width, epilogue=functools.partial(_epi_k, hd=hd), extras=[gk],
                       out_kinds=["heads", "bf16"], name="proj_k")
        v, vbt = proj(2 * sb_width, sb_width, epilogue=functools.partial(_epi_v, hd=hd), extras=[],
                      out_kinds=["heads", "bf16_t"], name="proj_v")
        (zx,) = proj(off_z, inner + conv_dim, epilogue=_epi_plain, extras=[], out_kinds=["f32"], name="proj_zx")
        (tail,) = _proj(xn, w_in_t, ncols=COL_TILE + 2 * d_model, tn=COL_TILE, w_rows=heads, w_block=tail_block,
                        epilogue=_epi_plain, extras=[], out_kinds=["f32"], hd=hd, name="proj_tail",
                        first_tile_rows=heads)
        return qn, kn, knb, v, vbt, zx, tail

    def merge_branches(x2, o_a, o_b, tail):
        merged = _merge(o_a, o_b, wa_b, wb_b, tail, off_ga=tail_ga, off_gb=tail_gb)
        return _resid_matmul_norm(x2, merged, wout_b, g_ffn[0])

    def ffn_down_and_embed(x1, act, p2):
        return _ple(_resid_matmul(x1, act, wdown_b), p2, g_ple[0], wple_b, wgate_b)

    xp2 = x_prompt.reshape(batch * seq, d_model)
    xs2 = x_sample.reshape(db, d_model)
    qn, kn, knb, v, vbt, zx, tail = mixer_inputs(xp2)
    qn_s, kn_s, _, v_s, _, zx_s, tail_s = mixer_inputs(xs2)
    n_pages = page_table.shape[1]

    o_a = _sb_prompt(qn, knb, vbt, sb_bias[0], batch=batch, seq=seq, heads=sb_heads, hd=hd)
    o_b, ssm_p, z_s = _ssd_prompt(zx, tail[:, :heads], conv_w[0], conv_b[0], dt_bias[0], a_log[0], d_skip[0],
                                  g_ssm[0], qn_s, cache_k, page_table, batch=batch, seq=seq, dims=dims)
    w_s = _sb_weights(z_s, sb_bias[0])
    cw = conv_w.shape[1]
    conv_p = zx.reshape(batch, seq, -1)[:, seq - (cw - 1):, inner:]
    x1, xn1 = merge_branches(xp2, o_a, o_b, tail)
    act, ffn_p, o_a_s = _ffn_up_prompt(xn1, wup_b, ffn_conv_w[0], ffn_conv_b[0], w_s, cache_v, page_table, seq=seq)
    yp = ffn_down_and_embed(x1, act, p_prompt[0].reshape(batch * seq, -1))
    k_p = kn.reshape(batch, seq, sb_heads, hd)
    v_p = v.reshape(batch, seq, sb_heads, hd)

    o_b_s, conv_s, ssm_s = _ssd_sample(
        zx_s[:, inner:], zx_s[:, :inner], tail_s[:, :heads], state_conv[0], state_ssm[0],
        conv_w[0], conv_b[0], dt_bias[0], a_log[0], d_skip[0], g_ssm[0], dims=dims)
    x1_s, xn1_s = merge_branches(xs2, o_a_s, o_b_s, tail_s)
    act_s, ffn_s = _ffn_up_sample(xn1_s, wup_b, ffn_conv_w[0], ffn_conv_b[0], state_ffn_conv[0])
    ys = ffn_down_and_embed(x1_s, act_s, p_sample[0].reshape(db, -1))

    return (yp.reshape(batch, seq, d_model), ys.reshape(db, 1, d_model),
            k_p[None], v_p[None], ssm_p[None], conv_p[None], ffn_p[None],
            kn_s.reshape(db, 1, sb_heads, hd)[None], v_s.reshape(db, 1, sb_heads, hd)[None],
            ssm_s[None], conv_s[None], ffn_s[None])
```
